```python
import math, functools
import jax, jax.numpy as jnp
from jax import lax
import numpy as np

D_MODEL = 1024
BATCH = 4
SEQ = 4096
DEPTH = 2
DEC_BATCH = 32
DEC_SEQ = 4
PAST_LEN = 16384
PAGE_SIZE = 128

N_HEADS = 8
HEAD_DIM = 64
V_DIM = 2 * HEAD_DIM
D_QK = N_HEADS * 2 * HEAD_DIM
D_ATT = N_HEADS * V_DIM
ATT_SCALE = HEAD_DIM ** -0.5
D_RNN = 1024
N_BLOCKS = 8
BLOCK_W = D_RNN // N_BLOCKS
CONV_W = 4
RG_C = 8.0
N_BUCKETS = 32
MAX_DIST = 128
Q_BLOCK = 128
EPS = 1e-6
IN_WIDTHS = (D_QK, D_QK, D_ATT, D_ATT, D_RNN, D_RNN, D_MODEL, D_MODEL)
D_IN = sum(IN_WIDTHS)
SPLIT_POINTS = tuple(int(s) for s in np.cumsum(IN_WIDTHS)[:-1])

kernel_name = 'hybrid_diffattn_rglru_decode_step'


def rmsnorm(x, g):
    xf = x.astype(jnp.float32)
    xf = xf * lax.rsqrt(jnp.mean(xf * xf, axis=-1, keepdims=True) + EPS)
    return (xf * g.astype(jnp.float32)).astype(x.dtype)


def t5_bucket(dist):
    n = jnp.maximum(dist, 0)
    max_exact = N_BUCKETS // 2
    nf = jnp.maximum(n, 1).astype(jnp.float32)
    large = max_exact + (jnp.log(nf / max_exact) / math.log(MAX_DIST / max_exact)
                         * (N_BUCKETS - max_exact)).astype(jnp.int32)
    large = jnp.minimum(large, N_BUCKETS - 1)
    return jnp.where(n < max_exact, n, large)


def rel_bias_from_dist(dist, rel_bias):
    return jnp.moveaxis(rel_bias[t5_bucket(dist)], -1, 0).astype(jnp.float32)


def diff_attn_core(q, k, v, bias, valid, lam):
    def probs(qa, ka):
        s = jnp.einsum('...qhd,...khd->...hqk', qa, ka).astype(jnp.float32) * ATT_SCALE + bias
        s = jnp.where(valid, s, -jnp.inf)
        return jax.nn.softmax(s, axis=-1)
    w = probs(q[..., :HEAD_DIM], k[..., :HEAD_DIM]) - lam * probs(q[..., HEAD_DIM:], k[..., HEAD_DIM:])
    return jnp.einsum('...hqk,...khv->...qhv', w.astype(v.dtype), v)


def diff_attn_prompt(q, k, v, lam, rel_bias):
    B, S = q.shape[0], q.shape[1]
    nb = S // Q_BLOCK
    k_pos = jnp.arange(S)
    q_blocks = jnp.moveaxis(q.reshape(B, nb, Q_BLOCK, N_HEADS, 2 * HEAD_DIM), 1, 0)

    def one_block(args):
        q_blk, start = args
        q_pos = start + jnp.arange(Q_BLOCK)
        dist = q_pos[:, None] - k_pos[None, :]
        return diff_attn_core(q_blk, k, v, rel_bias_from_dist(dist, rel_bias), dist >= 0, lam)

    o = lax.map(one_block, (q_blocks, jnp.arange(nb) * Q_BLOCK))
    return jnp.moveaxis(o, 0, 1).reshape(B, S, N_HEADS, V_DIM)


def diff_attn_sample(q, k, v, lam, rel_bias, cache_k_l, cache_v_l, page_table):
    T = q.shape[1]
    past = page_table.shape[1] * PAGE_SIZE
    q_pos = past + jnp.arange(T)
    k_pos = jnp.arange(past + T)
    dist = q_pos[:, None] - k_pos[None, :]
    bias = rel_bias_from_dist(dist, rel_bias)
    valid = dist >= 0

    def one_seq(args):
        q_s, k_s, v_s, pages = args
        k_all = jnp.concatenate([cache_k_l[pages].reshape(past, N_HEADS, 2 * HEAD_DIM), k_s], axis=0)
        v_all = jnp.concatenate([cache_v_l[pages].reshape(past, N_HEADS, V_DIM), v_s], axis=0)
        return diff_attn_core(q_s, k_all, v_all, bias, valid, lam)

    return lax.map(one_seq, (q, k, v, page_table))


def causal_conv(x, buf, w, b):
    T = x.shape[1]
    xp = jnp.concatenate([buf, x], axis=1)
    y = b + sum(xp[:, j:j + T] * w[j] for j in range(CONV_W))
    return y, xp[:, -(CONV_W - 1):]


def rglru(x, h0, w_a, b_a, w_i, b_i, lru_lam):
    B, T, _ = x.shape
    xf = x.astype(jnp.float32)
    xb = xf.reshape(B, T, N_BLOCKS, BLOCK_W)
    r = jax.nn.sigmoid(jnp.einsum('btnc,ncd->btnd', xb, w_a.astype(jnp.float32)).reshape(B, T, D_RNN) + b_a)
    i = jax.nn.sigmoid(jnp.einsum('btnc,ncd->btnd', xb, w_i.astype(jnp.float32)).reshape(B, T, D_RNN) + b_i)
    log_a = -RG_C * r * jax.nn.softplus(-lru_lam.astype(jnp.float32))
    a = jnp.exp(log_a)
    u = jnp.sqrt(-jnp.expm1(2.0 * log_a)) * (i * xf)
    u = u.at[:, 0].add(a[:, 0] * h0.astype(jnp.float32))

    def combine(left, right):
        a_l, u_l = left
        a_r, u_r = right
        return a_l * a_r, a_r * u_l + u_r

    _, h = lax.associative_scan(combine, (a, u), axis=1)
    h = h.astype(x.dtype)
    return h, h[:, -1]


def trunk_layer(x, conv_buf, h0, attend, lam_init, norm_g, w_in, lam_vecs, subln_g,
                conv_w, conv_b, w_a, b_a, w_i, b_i, lru_lam, w_pa, w_pr, w_out):
    B, T, _ = x.shape
    u = rmsnorm(x, norm_g)
    proj = u @ w_in
    q, k, v, z_a, x_r, z_r, g_a, g_r = jnp.split(proj, SPLIT_POINTS, axis=-1)
    q = q.reshape(B, T, N_HEADS, 2 * HEAD_DIM)
    k = k.reshape(B, T, N_HEADS, 2 * HEAD_DIM)
    v = v.reshape(B, T, N_HEADS, V_DIM)
    lv = lam_vecs.astype(jnp.float32)
    lam = jnp.exp(jnp.sum(lv[0] * lv[1])) - jnp.exp(jnp.sum(lv[2] * lv[3])) + lam_init
    o = attend(q, k, v, lam)
    o = rmsnorm(o, subln_g) * (1.0 - lam_init)
    branch_a = (o.reshape(B, T, D_ATT) * jax.nn.silu(z_a)) @ w_pa
    xc, new_buf = causal_conv(x_r, conv_buf, conv_w, conv_b)
    h, h_last = rglru(xc, h0, w_a, b_a, w_i, b_i, lru_lam)
    branch_r = (h * jax.nn.silu(z_r)) @ w_pr
    merged = jax.nn.sigmoid(g_a) * branch_a + jax.nn.sigmoid(g_r) * branch_r
    y = x + merged @ w_out
    return y, k, v, new_buf, h_last


def setup_inputs(seed: int = 0) -> dict:
    key = jax.random.key(seed)
    ks = jax.random.split(key, 32)
    f32 = jnp.float32
    n_pages = PAST_LEN // PAGE_SIZE
    n_pool = (DEC_BATCH * n_pages * 5) // 4
    nrm = lambda k, shape, s: jax.random.normal(k, shape, f32) * s
    u01 = jax.random.uniform(ks[20], (DEPTH, D_RNN), f32, 0.9, 0.999)
    sig_l = u01 ** (1.0 / RG_C)
    lru_lam = jnp.log(sig_l) - jnp.log1p(-sig_l)
    page_table = jax.random.permutation(ks[4], n_pool)[:DEC_BATCH * n_pages].reshape(DEC_BATCH, n_pages).astype(jnp.int32)
    return {
        'x_prompt': nrm(ks[0], (BATCH, SEQ, D_MODEL), 1.0),
        'x_sample': nrm(ks[1], (DEC_BATCH, DEC_SEQ, D_MODEL), 1.0),
        'cache_k': nrm(ks[2], (DEPTH, n_pool, PAGE_SIZE, N_HEADS, 2 * HEAD_DIM), 1.0),
        'cache_v': nrm(ks[3], (DEPTH, n_pool, PAGE_SIZE, N_HEADS, V_DIM), 1.0),
        'state_conv': nrm(ks[5], (DEPTH, DEC_BATCH, CONV_W - 1, D_RNN), 1.0),
        'state_h': nrm(ks[6], (DEPTH, DEC_BATCH, D_RNN), 0.5),
        'page_table': page_table,
        'rel_bias': nrm(ks[7], (N_BUCKETS, N_HEADS), 0.5),
        'norm_g': 1.0 + nrm(ks[8], (DEPTH, D_MODEL), 0.02),
        'w_in': nrm(ks[9], (DEPTH, D_MODEL, D_IN), D_MODEL ** -0.5),
        'lam_vecs': nrm(ks[10], (DEPTH, 4, HEAD_DIM), 0.1),
        'subln_g': 1.0 + nrm(ks[11], (DEPTH, V_DIM), 0.02),
        'conv_w': nrm(ks[12], (DEPTH, CONV_W, D_RNN), CONV_W ** -0.5),
        'conv_b': nrm(ks[13], (DEPTH, D_RNN), 0.02),
        'w_a': nrm(ks[14], (DEPTH, N_BLOCKS, BLOCK_W, BLOCK_W), BLOCK_W ** -0.5),
        'b_a': nrm(ks[15], (DEPTH, D_RNN), 0.02),
        'w_i': nrm(ks[16], (DEPTH, N_BLOCKS, BLOCK_W, BLOCK_W), BLOCK_W ** -0.5),
        'b_i': nrm(ks[17], (DEPTH, D_RNN), 0.02),
        'lru_lam': lru_lam,
        'w_pa': nrm(ks[18], (DEPTH, D_ATT, D_MODEL), D_ATT ** -0.5),
        'w_pr': nrm(ks[19], (DEPTH, D_RNN, D_MODEL), D_RNN ** -0.5),
        'w_out': nrm(ks[21], (DEPTH, D_MODEL, D_MODEL), D_MODEL ** -0.5),
        'final_norm_g': 1.0 + nrm(ks[22], (D_MODEL,), 0.02),
    }


def reference(x_prompt, x_sample, cache_k, cache_v, state_conv, state_h, page_table, rel_bias,
              norm_g, w_in, lam_vecs, subln_g, conv_w, conv_b, w_a, b_a, w_i, b_i, lru_lam,
              w_pa, w_pr, w_out, final_norm_g):
    yp, ys = x_prompt, x_sample
    B = x_prompt.shape[0]
    kp, vp, cp, hp, ksm, vsm, csm, hsm = [], [], [], [], [], [], [], []
    for l in range(DEPTH):
        lam_init = 0.8 - 0.6 * math.exp(-0.3 * l)
        params = (norm_g[l], w_in[l], lam_vecs[l], subln_g[l], conv_w[l], conv_b[l],
                  w_a[l], b_a[l], w_i[l], b_i[l], lru_lam[l], w_pa[l], w_pr[l], w_out[l])
        attend_p = functools.partial(diff_attn_prompt, rel_bias=rel_bias)
        attend_s = functools.partial(diff_attn_sample, rel_bias=rel_bias, cache_k_l=cache_k[l],
                                     cache_v_l=cache_v[l], page_table=page_table)
        buf0 = jnp.zeros((B, CONV_W - 1, D_RNN), x_prompt.dtype)
        h00 = jnp.zeros((B, D_RNN), x_prompt.dtype)
        yp, k_l, v_l, c_l, h_l = trunk_layer(yp, buf0, h00, attend_p, lam_init, *params)
        kp.append(k_l); vp.append(v_l); cp.append(c_l); hp.append(h_l)
        ys, k_l, v_l, c_l, h_l = trunk_layer(ys, state_conv[l], state_h[l], attend_s, lam_init, *params)
        ksm.append(k_l); vsm.append(v_l); csm.append(c_l); hsm.append(h_l)
    y_prompt = rmsnorm(yp, final_norm_g)
    y_sample = rmsnorm(ys, final_norm_g)
    return (y_prompt, y_sample, jnp.stack(kp), jnp.stack(vp), jnp.stack(cp), jnp.stack(hp),
            jnp.stack(ksm), jnp.stack(vsm), jnp.stack(csm), jnp.stack(hsm))
```

```python
import functools
import math

import jax
import jax.numpy as jnp
from jax import lax
from jax.experimental import pallas as pl
from jax.experimental.pallas import tpu as pltpu

F32 = jnp.float32
BF16 = jnp.bfloat16

D_MODEL = 1024
N_HEADS = 8
HEAD_DIM = 64
V_DIM = 2 * HEAD_DIM
D_RNN = 1024
N_BLOCKS = 8
BLOCK_W = D_RNN // N_BLOCKS
CONV_W = 4
RG_C = 8.0
N_BUCKETS = 32
MAX_DIST = 128
PAGE_SIZE = 128
EPS = 1e-6
ATT_SCALE = HEAD_DIM ** -0.5
N_PROJ = 8
COL_Q, COL_K, COL_V, COL_ZA, COL_XR, COL_ZR, COL_GA, COL_GR = range(N_PROJ)
MASKED = -1e30

ATT_TILE = 256
POST_TILE = 256
PROJ_TILE = 512
PAGES_PER_STEP = 8
VMEM_LIMIT = 56 * 1024 * 1024


def _params(n_axes):
    return pltpu.CompilerParams(dimension_semantics=("arbitrary",) * n_axes,
                                vmem_limit_bytes=VMEM_LIMIT)


def _sigmoid(x):
    return 1.0 / (1.0 + jnp.exp(-x))


def _silu(x):
    return x * _sigmoid(x)


def _softplus(x):
    return jnp.maximum(x, 0.0) + jnp.log1p(jnp.exp(-jnp.abs(x)))


def _diff_lambda(lv_ref, lam_init):
    a = jnp.sum(lv_ref[0:1, :] * lv_ref[1:2, :], axis=1, keepdims=True)
    b = jnp.sum(lv_ref[2:3, :] * lv_ref[3:4, :], axis=1, keepdims=True)
    return jnp.exp(a) - jnp.exp(b) + lam_init


def _split_maps(q):
    lane = lax.broadcasted_iota(jnp.int32, q.shape, 1)
    qs = q * jnp.asarray(ATT_SCALE, q.dtype)
    zero = jnp.zeros_like(qs)
    return jnp.concatenate([jnp.where(lane < HEAD_DIM, qs, zero),
                            jnp.where(lane >= HEAD_DIM, qs, zero)], axis=0)


def _subln(d, g, lam_init):
    ms = jnp.mean(d * d, axis=-1, keepdims=True)
    return d * lax.rsqrt(ms + EPS) * g * (1.0 - lam_init)


def _inproj_kernel(x_ref, g_ref, w_ref, p_ref, kf_ref, vf_ref, u_scr):
    j = pl.program_id(1)

    @pl.when(j == 0)
    def _():
        x = x_ref[...]
        ms = jnp.mean(x * x, axis=-1, keepdims=True)
        u_scr[...] = (x * lax.rsqrt(ms + EPS) * g_ref[...]).astype(BF16)

    acc = jnp.dot(u_scr[...], w_ref[...], preferred_element_type=F32)
    p_ref[...] = acc.astype(BF16)

    @pl.when(j == COL_K)
    def _():
        kf_ref[...] = acc

    @pl.when(j == COL_V)
    def _():
        vf_ref[...] = acc


def _inproj(x, g, w_bf, tile):
    n = x.shape[0]
    return pl.pallas_call(
        _inproj_kernel,
        grid=(n // tile, N_PROJ),
        in_specs=[
            pl.BlockSpec((tile, D_MODEL), lambda i, j: (i, 0)),
            pl.BlockSpec((1, D_MODEL), lambda i, j: (0, 0)),
            pl.BlockSpec((D_MODEL, D_MODEL), lambda i, j: (0, j)),
        ],
        out_specs=[
            pl.BlockSpec((tile, D_MODEL), lambda i, j: (i, j)),
            pl.BlockSpec((tile, D_MODEL), lambda i, j: (i, 0)),
            pl.BlockSpec((tile, D_MODEL), lambda i, j: (i, 0)),
        ],
        out_shape=[
            jax.ShapeDtypeStruct((n, N_PROJ * D_MODEL), BF16),
            jax.ShapeDtypeStruct((n, D_MODEL), F32),
            jax.ShapeDtypeStruct((n, D_MODEL), F32),
        ],
        scratch_shapes=[pltpu.VMEM((tile, D_MODEL), BF16)],
        compiler_params=_params(2),
        name="inproj",
    )(x, g, w_bf)


def _attn_prompt_kernel(cfar_ref, q_ref, k_ref, v_ref, bd_ref, bs_ref, lv_ref, sg_ref,
                        o_ref, m_scr, l_scr, acc_scr, *, lam_init):
    T = ATT_TILE
    qi = pl.program_id(2)
    qq = _split_maps(q_ref[...])
    m_scr[...] = jnp.full(m_scr.shape, MASKED, F32)
    l_scr[...] = jnp.zeros(l_scr.shape, F32)
    acc_scr[...] = jnp.zeros(acc_scr.shape, F32)

    def step(kj, bias):
        start = pl.multiple_of(kj * T, T)
        kt = k_ref[pl.ds(start, T), :]
        vt = v_ref[pl.ds(start, T), :]
        s = lax.dot_general(qq, kt, (((1,), (1,)), ((), ())), preferred_element_type=F32)
        s = s + bias
        m_old = m_scr[...]
        m_new = jnp.maximum(m_old, jnp.max(s, axis=-1, keepdims=True))
        alpha = jnp.exp(m_old - m_new)
        p = jnp.exp(s - m_new)
        l_scr[...] = alpha * l_scr[...] + jnp.sum(p, axis=-1, keepdims=True)
        acc_scr[...] = alpha * acc_scr[...] + jnp.dot(p.astype(BF16), vt, preferred_element_type=F32)
        m_scr[...] = m_new

    c_far = cfar_ref[...]

    def far_body(kj, carry):
        step(kj, c_far)
        return carry

    lax.fori_loop(0, qi - 1, far_body, 0)

    @pl.when(qi >= 1)
    def _():
        b = bs_ref[...]
        step(qi - 1, jnp.concatenate([b, b], axis=0))

    b = bd_ref[...]
    step(qi, jnp.concatenate([b, b], axis=0))

    o = acc_scr[...] / l_scr[...]
    lam = _diff_lambda(lv_ref, lam_init)
    d = o[:T] - lam * o[T:]
    o_ref[...] = _subln(d, sg_ref[...], lam_init).astype(o_ref.dtype)


def _attn_prompt(p, cfar, bias_diag, bias_sub, lam_vecs, subln_g, lam_init, batch, seq):
    T = ATT_TILE
    nq = seq // T
    return pl.pallas_call(
        functools.partial(_attn_prompt_kernel, lam_init=lam_init),
        grid=(batch, N_HEADS, nq),
        in_specs=[
            pl.BlockSpec((None, 1, 1), lambda b, h, i: (h, 0, 0)),
            pl.BlockSpec((T, V_DIM), lambda b, h, i: (b * nq + i, COL_Q * N_HEADS + h)),
            pl.BlockSpec((seq, V_DIM), lambda b, h, i: (b, COL_K * N_HEADS + h)),
            pl.BlockSpec((seq, V_DIM), lambda b, h, i: (b, COL_V * N_HEADS + h)),
            pl.BlockSpec((None, T, T), lambda b, h, i: (h, 0, 0)),
            pl.BlockSpec((None, T, T), lambda b, h, i: (h, 0, 0)),
            pl.BlockSpec((4, HEAD_DIM), lambda b, h, i: (0, 0)),
            pl.BlockSpec((1, V_DIM), lambda b, h, i: (0, 0)),
        ],
        out_specs=pl.BlockSpec((T, V_DIM), lambda b, h, i: (b * nq + i, h)),
        out_shape=jax.ShapeDtypeStruct((batch * seq, N_HEADS * V_DIM), BF16),
        scratch_shapes=[pltpu.VMEM((2 * T, 1), F32), pltpu.VMEM((2 * T, 1), F32),
                        pltpu.VMEM((2 * T, V_DIM), F32)],
        compiler_params=_params(3),
        name="attn_prompt",
    )(cfar.reshape(N_HEADS, 1, 1), p, p, p, bias_diag, bias_sub, lam_vecs, subln_g)


def _attn_decode_kernel(pt_ref, q_ref, kn_ref, vn_ref, bfar_ref, blast_ref, bnew_ref, lv_ref, sg_ref,
                        *rest, lam_init):
    G = PAGES_PER_STEP
    k_refs = rest[:G]
    v_refs = rest[G:2 * G]
    o_ref = rest[2 * G]
    wq_scr, m_scr, l_scr, acc_scr = rest[2 * G + 1:]
    c = pl.program_id(1)
    n_c = pl.num_programs(1)
    rows = PAGE_SIZE * N_HEADS

    @pl.when(c == 0)
    def _():
        wq_scr[...] = _split_maps(q_ref[...])
        m_scr[...] = jnp.full(m_scr.shape, MASKED, F32)
        l_scr[...] = jnp.zeros(l_scr.shape, F32)
        acc_scr[...] = jnp.zeros(acc_scr.shape, F32)

    wq = wq_scr[...]

    def update(k2d, v2d, bias):
        s = lax.dot_general(wq, k2d, (((1,), (1,)), ((), ())), preferred_element_type=F32)
        s = s + bias
        m_old = m_scr[...]
        m_new = jnp.maximum(m_old, jnp.max(s, axis=-1, keepdims=True))
        alpha = jnp.exp(m_old - m_new)
        p = jnp.exp(s - m_new)
        l_scr[...] = alpha * l_scr[...] + jnp.sum(p, axis=-1, keepdims=True)
        acc_scr[...] = alpha * acc_scr[...] + jnp.dot(p.astype(BF16), v2d, preferred_element_type=F32)
        m_scr[...] = m_new

    for g in range(G):
        k2d = k_refs[g][...].reshape(rows, V_DIM).astype(BF16)
        v2d = v_refs[g][...].reshape(rows, V_DIM).astype(BF16)
        if g == G - 1:
            bias = jnp.where(c == n_c - 1, blast_ref[...], bfar_ref[...])
        else:
            bias = bfar_ref[...]
        update(k2d, v2d, bias)

    @pl.when(c == n_c - 1)
    def _():
        update(kn_ref[...], vn_ref[...], bnew_ref[...])
        o = acc_scr[...] / l_scr[...]
        half = o.shape[0] // 2
        lam = _diff_lambda(lv_ref, lam_init)
        d = o[:half] - lam * o[half:]
        o_ref[...] = _subln(d, sg_ref[...], lam_init).astype(o_ref.dtype)


def _attn_decode(layer, page_table, q, k_new, v_new, cache_k, cache_v, bias_far, bias_last, bias_new,
                 lam_vecs, subln_g, lam_init):
    G = PAGES_PER_STEP
    n_seq, n_pages = page_table.shape
    tok_heads = q.shape[1]
    rows = PAGE_SIZE * N_HEADS

    def page_spec(g):
        return pl.BlockSpec((None, None, PAGE_SIZE, N_HEADS, V_DIM),
                            lambda s, c, pt: (layer, pt[s, c * G + g], 0, 0, 0))

    def seq_spec(r):
        return pl.BlockSpec((None, r, V_DIM), lambda s, c, pt: (s, 0, 0))

    def const_spec(shape):
        return pl.BlockSpec(shape, lambda s, c, pt: (0,) * len(shape))

    grid_spec = pltpu.PrefetchScalarGridSpec(
        num_scalar_prefetch=1,
        grid=(n_seq, n_pages // G),
        in_specs=[seq_spec(tok_heads), seq_spec(tok_heads), seq_spec(tok_heads),
                  const_spec((2 * tok_heads, rows)), const_spec((2 * tok_heads, rows)),
                  const_spec((2 * tok_heads, tok_heads)),
                  const_spec((4, HEAD_DIM)), const_spec((1, V_DIM))]
                 + [page_spec(g) for g in range(G)] + [page_spec(g) for g in range(G)],
        out_specs=seq_spec(tok_heads),
        scratch_shapes=[pltpu.VMEM((2 * tok_heads, V_DIM), BF16),
                        pltpu.VMEM((2 * tok_heads, 1), F32), pltpu.VMEM((2 * tok_heads, 1), F32),
                        pltpu.VMEM((2 * tok_heads, V_DIM), F32)],
    )
    return pl.pallas_call(
        functools.partial(_attn_decode_kernel, lam_init=lam_init),
        grid_spec=grid_spec,
        out_shape=jax.ShapeDtypeStruct((n_seq, tok_heads, V_DIM), BF16),
        compiler_params=_params(2),
        name="attn_decode",
    )(page_table, q, k_new, v_new, bias_far, bias_last, bias_new, lam_vecs, subln_g,
      *([cache_k] * G), *([cache_v] * G))


def _gates(xc, wai_ref, ba_ref, bi_ref):
    xb = xc.astype(BF16)
    r_parts, i_parts = [], []
    for n in range(N_BLOCKS):
        sl = slice(n * BLOCK_W, (n + 1) * BLOCK_W)
        g = jnp.dot(xb[:, sl], wai_ref[n], preferred_element_type=F32)
        r_parts.append(g[:, :BLOCK_W])
        i_parts.append(g[:, BLOCK_W:])
    r = _sigmoid(jnp.concatenate(r_parts, axis=1) + ba_ref[...])
    i = _sigmoid(jnp.concatenate(i_parts, axis=1) + bi_ref[...])
    return r, i


def _lru_coeffs(xc, r, i, lam_ref):
    log_a = -RG_C * r * _softplus(-lam_ref[...])
    a = jnp.exp(log_a)
    u = jnp.sqrt(1.0 - a * a) * (i * xc)
    return a, u


def _merge_out(x, o, za, h, zr, ga, gr, wpa_ref, wpr_ref, wout_ref):
    branch_a = jnp.dot((o * _silu(za)).astype(BF16), wpa_ref[...], preferred_element_type=F32)
    branch_r = jnp.dot((h * _silu(zr)).astype(BF16), wpr_ref[...], preferred_element_type=F32)
    merged = _sigmoid(ga) * branch_a + _sigmoid(gr) * branch_r
    return x + jnp.dot(merged.astype(BF16), wout_ref[...], preferred_element_type=F32)


def _final_norm(y, g_ref):
    ms = jnp.mean(y * y, axis=-1, keepdims=True)
    return y * lax.rsqrt(ms + EPS) * g_ref[...]


def _post_prompt_kernel(x_ref, o_ref, za_ref, xr_ref, zr_ref, ga_ref, gr_ref,
                        cw_ref, cb_ref, wai_ref, ba_ref, bi_ref, lam_ref,
                        wpa_ref, wpr_ref, wout_ref, fg_ref,
                        y_ref, conv_ref, h_ref, xprev_scr, hprev_scr, *, final):
    T = POST_TILE
    t = pl.program_id(1)

    @pl.when(t == 0)
    def _():
        xprev_scr[...] = jnp.zeros(xprev_scr.shape, F32)
        hprev_scr[...] = jnp.zeros(hprev_scr.shape, F32)

    xr = xr_ref[...].astype(F32)
    row8 = lax.broadcasted_iota(jnp.int32, (8, D_RNN), 0)
    xprev = xprev_scr[...]

    def delayed(k):
        r = pltpu.roll(xr, k, 0)
        top = jnp.where(row8 < k, pltpu.roll(xprev, k, 0), r[0:8])
        return jnp.concatenate([top, r[8:]], axis=0)

    xc = cb_ref[...] + cw_ref[3:4, :] * xr
    for k in range(1, CONV_W):
        xc = xc + cw_ref[CONV_W - 1 - k:CONV_W - k, :] * delayed(k)

    r, i = _gates(xc, wai_ref, ba_ref, bi_ref)
    a, u = _lru_coeffs(xc, r, i, lam_ref)

    row = lax.broadcasted_iota(jnp.int32, (T, D_RNN), 0)
    u = jnp.where(row == 0, u + a * hprev_scr[7:8, :], u)
    d = 1
    while d < T:
        keep = row >= d
        u = jnp.where(keep, a * pltpu.roll(u, d, 0) + u, u)
        a = jnp.where(keep, a * pltpu.roll(a, d, 0), a)
        d *= 2
    h = u

    xprev_scr[...] = xr[T - 8:, :]
    hprev_scr[...] = h[T - 8:, :]
    conv_ref[...] = xr[T - 8:, :]
    h_ref[...] = h[T - 8:, :]

    y = _merge_out(x_ref[...], o_ref[...].astype(F32), za_ref[...].astype(F32), h,
                   zr_ref[...].astype(F32), ga_ref[...].astype(F32), gr_ref[...].astype(F32),
                   wpa_ref, wpr_ref, wout_ref)
    if final:
        y = _final_norm(y, fg_ref)
    y_ref[...] = y


def _post_prompt(x, o, p, weights, final, batch, seq):
    T = POST_TILE
    nt = seq // T

    def col_spec(col):
        return pl.BlockSpec((T, D_MODEL), lambda b, t: (b * nt + t, col))

    def const_spec(shape):
        return pl.BlockSpec(shape, lambda b, t: (0,) * len(shape))

    tail_spec = pl.BlockSpec((None, 8, D_RNN), lambda b, t: (b, 0, 0))
    (cw, cb, wai, ba, bi, lam, wpa, wpr, wout, fg) = weights
    return pl.pallas_call(
        functools.partial(_post_prompt_kernel, final=final),
        grid=(batch, nt),
        in_specs=[col_spec(0), col_spec(0), col_spec(COL_ZA), col_spec(COL_XR), col_spec(COL_ZR),
                  col_spec(COL_GA), col_spec(COL_GR),
                  const_spec(cw.shape), const_spec(cb.shape), const_spec(wai.shape),
                  const_spec(ba.shape), const_spec(bi.shape), const_spec(lam.shape),
                  const_spec(wpa.shape), const_spec(wpr.shape), const_spec(wout.shape),
                  const_spec(fg.shape)],
        out_specs=[col_spec(0), tail_spec, tail_spec],
        out_shape=[jax.ShapeDtypeStruct((batch * seq, D_MODEL), F32),
                   jax.ShapeDtypeStruct((batch, 8, D_RNN), F32),
                   jax.ShapeDtypeStruct((batch, 8, D_RNN), F32)],
        scratch_shapes=[pltpu.VMEM((8, D_RNN), F32), pltpu.VMEM((8, D_RNN), F32)],
        compiler_params=_params(2),
        name="post_prompt",
    )(x, o, p, p, p, p, p, cw, cb, wai, ba, bi, lam, wpa, wpr, wout, fg)


def _post_sample_kernel(x_ref, o_ref, za_ref, xr_ref, zr_ref, ga_ref, gr_ref, buf_ref, h0_ref,
                        cw_ref, cb_ref, wai_ref, ba_ref, bi_ref, lam_ref,
                        wpa_ref, wpr_ref, wout_ref, fg_ref,
                        y_ref, conv_ref, h_ref, *, final, n_seq, n_tok):
    xr = xr_ref[...].astype(F32)
    hist = [buf_ref[k] for k in range(CONV_W - 1)]
    hist += [xr[t * n_seq:(t + 1) * n_seq, :] for t in range(n_tok)]
    xc_parts = []
    for t in range(n_tok):
        acc = cb_ref[...]
        for j in range(CONV_W):
            acc = acc + cw_ref[j:j + 1, :] * hist[t + j]
        xc_parts.append(acc)
    xc = jnp.concatenate(xc_parts, axis=0)
    for k in range(CONV_W - 1):
        conv_ref[k] = hist[n_tok + k]

    r, i = _gates(xc, wai_ref, ba_ref, bi_ref)
    a, u = _lru_coeffs(xc, r, i, lam_ref)
    h = h0_ref[...]
    h_parts = []
    for t in range(n_tok):
        sl = slice(t * n_seq, (t + 1) * n_seq)
        h = a[sl, :] * h + u[sl, :]
        h_parts.append(h)
    h_ref[...] = h
    h_all = jnp.concatenate(h_parts, axis=0)

    y = _merge_out(x_ref[...], o_ref[...].astype(F32), za_ref[...].astype(F32), h_all,
                   zr_ref[...].astype(F32), ga_ref[...].astype(F32), gr_ref[...].astype(F32),
                   wpa_ref, wpr_ref, wout_ref)
    if final:
        y = _final_norm(y, fg_ref)
    y_ref[...] = y


def _post_sample(x, o, p, buf, h0, weights, final, n_seq, n_tok):
    n = n_seq * n_tok

    def col_spec(col):
        return pl.BlockSpec((n, D_MODEL), lambda i: (0, col))

    def const_spec(shape):
        return pl.BlockSpec(shape, lambda i: (0,) * len(shape))

    (cw, cb, wai, ba, bi, lam, wpa, wpr, wout, fg) = weights
    return pl.pallas_call(
        functools.partial(_post_sample_kernel, final=final, n_seq=n_seq, n_tok=n_tok),
        grid=(1,),
        in_specs=[col_spec(0), col_spec(0), col_spec(COL_ZA), col_spec(COL_XR), col_spec(COL_ZR),
                  col_spec(COL_GA), col_spec(COL_GR), const_spec(buf.shape), const_spec(h0.shape),
                  const_spec(cw.shape), const_spec(cb.shape), const_spec(wai.shape),
                  const_spec(ba.shape), const_spec(bi.shape), const_spec(lam.shape),
                  const_spec(wpa.shape), const_spec(wpr.shape), const_spec(wout.shape),
                  const_spec(fg.shape)],
        out_specs=[col_spec(0), const_spec(buf.shape), const_spec(h0.shape)],
        out_shape=[jax.ShapeDtypeStruct((n, D_MODEL), F32),
                   jax.ShapeDtypeStruct(buf.shape, F32),
                   jax.ShapeDtypeStruct(h0.shape, F32)],
        compiler_params=_params(1),
        name="post_sample",
    )(x, o, p, p, p, p, p, buf, h0, cw, cb, wai, ba, bi, lam, wpa, wpr, wout, fg)


def _t5_bucket(dist):
    n = jnp.maximum(dist, 0)
    max_exact = N_BUCKETS // 2
    nf = jnp.maximum(n, 1).astype(F32)
    large = max_exact + (jnp.log(nf / max_exact) / math.log(MAX_DIST / max_exact)
                         * (N_BUCKETS - max_exact)).astype(jnp.int32)
    large = jnp.minimum(large, N_BUCKETS - 1)
    return jnp.where(n < max_exact, n, large)


def _bias_of(dist, valid, rel_bias):
    b = jnp.moveaxis(rel_bias[_t5_bucket(dist)], -1, 0).astype(F32)
    return jnp.where(valid, b, MASKED)


def _prompt_bias(rel_bias):
    T = ATT_TILE
    r = jnp.arange(T)[:, None]
    c = jnp.arange(T)[None, :]
    diag = _bias_of(r - c, r >= c, rel_bias)
    sub = _bias_of(T + r - c, jnp.ones((T, T), bool), rel_bias)
    far = rel_bias[N_BUCKETS - 1].astype(F32)
    return far, diag, sub


def _decode_bias(rel_bias, n_tok):
    hq = jnp.tile(jnp.arange(N_HEADS), 2 * n_tok)[:, None]
    iq = jnp.tile(jnp.repeat(jnp.arange(n_tok), N_HEADS), 2)[:, None]

    def build(n_keys, dist_of):
        tk = jnp.repeat(jnp.arange(n_keys), N_HEADS)[None, :]
        hk = jnp.tile(jnp.arange(N_HEADS), n_keys)[None, :]
        dist = dist_of(iq, tk)
        b = rel_bias[_t5_bucket(dist), hq].astype(F32)
        return jnp.where((hq == hk) & (dist >= 0), b, MASKED)

    far = build(PAGE_SIZE, lambda i, t: jnp.full_like(i + t, 2 * MAX_DIST))
    last = build(PAGE_SIZE, lambda i, t: PAGE_SIZE + i - t)
    new = build(n_tok, lambda i, t: i - t)
    return far, last, new


def kernel(x_prompt, x_sample, cache_k, cache_v, state_conv, state_h, page_table, rel_bias,
           norm_g, w_in, lam_vecs, subln_g, conv_w, conv_b, w_a, b_a, w_i, b_i, lru_lam,
           w_pa, w_pr, w_out, final_norm_g):
    batch, seq, _ = x_prompt.shape
    n_seq, n_tok, _ = x_sample.shape
    depth = w_in.shape[0]
    assert seq % ATT_TILE == 0 and seq % POST_TILE == 0 and (batch * seq) % PROJ_TILE == 0
    assert page_table.shape[1] % PAGES_PER_STEP == 0 and cache_k.shape[2] == PAGE_SIZE
    assert ATT_TILE >= MAX_DIST and PAGE_SIZE >= MAX_DIST

    cfar, bias_diag, bias_sub = _prompt_bias(rel_bias)
    dec_far, dec_last, dec_new = _decode_bias(rel_bias, n_tok)

    yp = x_prompt.reshape(batch * seq, D_MODEL)
    ys = jnp.swapaxes(x_sample, 0, 1).reshape(n_tok * n_seq, D_MODEL)
    row = lambda v: v.reshape(1, -1)
    outs = [[] for _ in range(8)]

    for l in range(depth):
        lam_init = 0.8 - 0.6 * math.exp(-0.3 * l)
        final = l == depth - 1
        w_in_bf = w_in[l].astype(BF16)
        wai = jnp.concatenate([w_a[l], w_i[l]], axis=-1).astype(BF16)
        weights = (conv_w[l], row(conv_b[l]), wai, row(b_a[l]), row(b_i[l]), row(lru_lam[l]),
                   w_pa[l].astype(BF16), w_pr[l].astype(BF16), w_out[l].astype(BF16),
                   row(final_norm_g))
        g_in, sg = row(norm_g[l]), row(subln_g[l])

        p, kf, vf = _inproj(yp, g_in, w_in_bf, PROJ_TILE)
        o = _attn_prompt(p, cfar, bias_diag, bias_sub, lam_vecs[l], sg, lam_init, batch, seq)
        yp, conv_t, h_t = _post_prompt(yp, o, p, weights, final, batch, seq)
        outs[0].append(kf.reshape(batch, seq, N_HEADS, V_DIM))
        outs[1].append(vf.reshape(batch, seq, N_HEADS, V_DIM))
        outs[2].append(conv_t[:, 8 - (CONV_W - 1):, :])
        outs[3].append(h_t[:, 7, :])

        ps, kfs, vfs = _inproj(ys, g_in, w_in_bf, n_tok * n_seq)
        seq_major = lambda v: jnp.swapaxes(v.reshape(n_tok, n_seq, N_HEADS, V_DIM), 0, 1)
        q_s = seq_major(ps[:, :D_MODEL]).reshape(n_seq, n_tok * N_HEADS, V_DIM)
        k_s, v_s = seq_major(kfs), seq_major(vfs)
        o_s = _attn_decode(l, page_table, q_s,
                           k_s.reshape(n_seq, n_tok * N_HEADS, V_DIM).astype(BF16),
                           v_s.reshape(n_seq, n_tok * N_HEADS, V_DIM).astype(BF16),
                           cache_k, cache_v, dec_far, dec_last, dec_new, lam_vecs[l], sg, lam_init)
        o_s = jnp.swapaxes(o_s.reshape(n_seq, n_tok, D_MODEL), 0, 1).reshape(n_tok * n_seq, D_MODEL)
        ys, conv_s, h_s = _post_sample(ys, o_s, ps, jnp.swapaxes(state_conv[l], 0, 1), state_h[l],
                                       weights, final, n_seq, n_tok)
        outs[4].append(k_s)
        outs[5].append(v_s)
        outs[6].append(jnp.swapaxes(conv_s, 0, 1))
        outs[7].append(h_s)

    y_prompt = yp.reshape(batch, seq, D_MODEL)
    y_sample = jnp.swapaxes(ys.reshape(n_tok, n_seq, D_MODEL), 0, 1)
    return (y_prompt, y_sample) + tuple(jnp.stack(o) for o in outs)
```

```python
import functools
import math

import jax
import jax.numpy as jnp
from jax import lax
from jax.experimental import pallas as pl
from jax.experimental.pallas import tpu as pltpu

F32 = jnp.float32
BF16 = jnp.bfloat16

D_MODEL = 1024
N_HEADS = 8
HEAD_DIM = 64
V_DIM = 2 * HEAD_DIM
D_RNN = 1024
N_BLOCKS = 8
BLOCK_W = D_RNN // N_BLOCKS
CONV_W = 4
RG_C = 8.0
N_BUCKETS = 32
MAX_DIST = 128
PAGE_SIZE = 128
EPS = 1e-6
ATT_SCALE = HEAD_DIM ** -0.5
N_PROJ = 8
COL_Q, COL_K, COL_V, COL_ZA, COL_XR, COL_ZR, COL_GA, COL_GR = range(N_PROJ)
MASKED = -1e30

ATT_TQ = 1024
ATT_TK = 256
ONES_ROWS = 16
POST_TILE = 256
PROJ_TILE = 512
PAGES_PER_STEP = 8
VMEM_LIMIT = 56 * 1024 * 1024


def _params(n_axes):
    return pltpu.CompilerParams(dimension_semantics=("arbitrary",) * n_axes,
                                vmem_limit_bytes=VMEM_LIMIT)


def _sigmoid(x):
    return 1.0 / (1.0 + jnp.exp(-x))


def _silu(x):
    return x * _sigmoid(x)


def _softplus(x):
    return jnp.maximum(x, 0.0) + jnp.log1p(jnp.exp(-jnp.abs(x)))


def _diff_lambda(lv_ref, lam_init):
    a = jnp.sum(lv_ref[0:1, :] * lv_ref[1:2, :], axis=1, keepdims=True)
    b = jnp.sum(lv_ref[2:3, :] * lv_ref[3:4, :], axis=1, keepdims=True)
    return jnp.exp(a) - jnp.exp(b) + lam_init


def _split_maps(q):
    lane = lax.broadcasted_iota(jnp.int32, q.shape, 1)
    qs = q * jnp.asarray(ATT_SCALE, q.dtype)
    zero = jnp.zeros_like(qs)
    return jnp.concatenate([jnp.where(lane < HEAD_DIM, qs, zero),
                            jnp.where(lane >= HEAD_DIM, qs, zero)], axis=0)


def _subln(d, g, lam_init):
    ms = jnp.mean(d * d, axis=-1, keepdims=True)
    return d * lax.rsqrt(ms + EPS) * g * (1.0 - lam_init)


def _inproj_kernel(x_ref, g_ref, w_ref, p_ref, kf_ref, vf_ref, u_scr):
    j = pl.program_id(1)

    @pl.when(j == 0)
    def _():
        x = x_ref[...]
        ms = jnp.mean(x * x, axis=-1, keepdims=True)
        u_scr[...] = (x * lax.rsqrt(ms + EPS) * g_ref[...]).astype(BF16)

    acc = jnp.dot(u_scr[...], w_ref[...], preferred_element_type=F32)
    p_ref[...] = acc.astype(BF16)

    @pl.when(j == COL_K)
    def _():
        kf_ref[...] = acc

    @pl.when(j == COL_V)
    def _():
        vf_ref[...] = acc


def _inproj(x, g, w_bf, tile):
    n = x.shape[0]
    return pl.pallas_call(
        _inproj_kernel,
        grid=(n // tile, N_PROJ),
        in_specs=[
            pl.BlockSpec((tile, D_MODEL), lambda i, j: (i, 0)),
            pl.BlockSpec((1, D_MODEL), lambda i, j: (0, 0)),
            pl.BlockSpec((D_MODEL, D_MODEL), lambda i, j: (0, j)),
        ],
        out_specs=[
            pl.BlockSpec((tile, D_MODEL), lambda i, j: (i, j)),
            pl.BlockSpec((tile, D_MODEL), lambda i, j: (i, 0)),
            pl.BlockSpec((tile, D_MODEL), lambda i, j: (i, 0)),
        ],
        out_shape=[
            jax.ShapeDtypeStruct((n, N_PROJ * D_MODEL), BF16),
            jax.ShapeDtypeStruct((n, D_MODEL), F32),
            jax.ShapeDtypeStruct((n, D_MODEL), F32),
        ],
        scratch_shapes=[pltpu.VMEM((tile, D_MODEL), BF16)],
        compiler_params=_params(2),
        name="inproj",
    )(x, g, w_bf)


def _attn_prompt_kernel(cfar_ref, q_ref, k_ref, v_ref, bias_ref, lv_ref, sg_ref,
                        o_ref, vte_scr, m_scr, acc_scr, *, lam_init, seq):
    TQ, TK = ATT_TQ, ATT_TK
    ratio = TQ // TK

    for j in range(seq // TK):
        vt = v_ref[j * TK:(j + 1) * TK, :].astype(F32).T
        vte_scr[j, 0:V_DIM, :] = vt.astype(BF16)
        vte_scr[j, V_DIM:, :] = jnp.ones((ONES_ROWS, TK), BF16)

    lam = _diff_lambda(lv_ref, lam_init)
    c_far = cfar_ref[...]

    def q_tile(qi, carry):
        q_start = pl.multiple_of(qi * TQ, TQ)
        qq = _split_maps(q_ref[pl.ds(q_start, TQ), :])
        m_scr[...] = jnp.full(m_scr.shape, MASKED, F32)
        acc_scr[...] = jnp.zeros(acc_scr.shape, F32)

        def step(kj, bias_tile, lo):
            w = TQ - lo
            kt = k_ref[pl.ds(pl.multiple_of(kj * TK, TK), TK), :]
            qs = jnp.concatenate([qq[lo:TQ], qq[TQ + lo:]], axis=0)
            st = lax.dot_general(kt, qs, (((1,), (1,)), ((), ())), preferred_element_type=F32)
            m_old = jnp.concatenate([m_scr[0, :, lo:], m_scr[1, :, lo:]], axis=1)
            if bias_tile is None:
                m_new = jnp.maximum(m_old, jnp.max(st, axis=0, keepdims=True) + c_far)
                shift = m_new - c_far
            else:
                b = bias_tile[:, lo:]
                st = st + jnp.concatenate([b, b], axis=1)
                m_new = jnp.maximum(m_old, jnp.max(st, axis=0, keepdims=True))
                shift = m_new
            alpha = jnp.exp(m_old - m_new)
            p = jnp.exp(st - shift).astype(BF16)
            pv = jnp.dot(vte_scr[kj], p, preferred_element_type=F32)
            for mp in range(2):
                cols = slice(mp * w, (mp + 1) * w)
                acc_scr[mp, :, lo:] = alpha[:, cols] * acc_scr[mp, :, lo:] + pv[:, cols]
                m_scr[mp, :, lo:] = m_new[:, cols]

        def far_body(kj, c):
            step(kj, None, 0)
            return c

        first_near = qi * ratio - 1
        lax.fori_loop(0, first_near, far_body, 0)

        @pl.when(qi >= 1)
        def _():
            step(first_near, bias_ref[0], 0)

        for s in range(ratio):
            step(qi * ratio + s, bias_ref[1 + s], s * TK)

        o = [acc_scr[mp, 0:V_DIM, :] / acc_scr[mp, V_DIM:V_DIM + 1, :] for mp in range(2)]
        d = (o[0] - lam * o[1]).T
        o_ref[pl.ds(q_start, TQ), :] = _subln(d, sg_ref[...], lam_init).astype(o_ref.dtype)
        return carry

    lax.fori_loop(0, seq // TQ, q_tile, 0)


def _attn_prompt(p, cfar, bias_near, lam_vecs, subln_g, lam_init, batch, seq):
    TQ, TK = ATT_TQ, ATT_TK
    n_near = bias_near.shape[1]
    return pl.pallas_call(
        functools.partial(_attn_prompt_kernel, lam_init=lam_init, seq=seq),
        grid=(batch, N_HEADS),
        in_specs=[
            pl.BlockSpec((None, 1, 1), lambda b, h: (h, 0, 0)),
            pl.BlockSpec((seq, V_DIM), lambda b, h: (b, COL_Q * N_HEADS + h)),
            pl.BlockSpec((seq, V_DIM), lambda b, h: (b, COL_K * N_HEADS + h)),
            pl.BlockSpec((seq, V_DIM), lambda b, h: (b, COL_V * N_HEADS + h)),
            pl.BlockSpec((None, n_near, TK, TQ), lambda b, h: (h, 0, 0, 0)),
            pl.BlockSpec((4, HEAD_DIM), lambda b, h: (0, 0)),
            pl.BlockSpec((1, V_DIM), lambda b, h: (0, 0)),
        ],
        out_specs=pl.BlockSpec((seq, V_DIM), lambda b, h: (b, h)),
        out_shape=jax.ShapeDtypeStruct((batch * seq, N_HEADS * V_DIM), BF16),
        scratch_shapes=[pltpu.VMEM((seq // TK, V_DIM + ONES_ROWS, TK), BF16),
                        pltpu.VMEM((2, 1, TQ), F32),
                        pltpu.VMEM((2, V_DIM + ONES_ROWS, TQ), F32)],
        compiler_params=_params(2),
        name="attn_prompt",
    )(cfar.reshape(N_HEADS, 1, 1), p, p, p, bias_near, lam_vecs, subln_g)


def _attn_decode_kernel(pt_ref, q_ref, kn_ref, vn_ref, bfar_ref, blast_ref, bnew_ref, lv_ref, sg_ref,
                        *rest, lam_init):
    G = PAGES_PER_STEP
    k_refs = rest[:G]
    v_refs = rest[G:2 * G]
    o_ref = rest[2 * G]
    wq_scr, m_scr, l_scr, acc_scr = rest[2 * G + 1:]
    c = pl.program_id(1)
    n_c = pl.num_programs(1)
    rows = PAGE_SIZE * N_HEADS

    @pl.when(c == 0)
    def _():
        wq_scr[...] = _split_maps(q_ref[...])
        m_scr[...] = jnp.full(m_scr.shape, MASKED, F32)
        l_scr[...] = jnp.zeros(l_scr.shape, F32)
        acc_scr[...] = jnp.zeros(acc_scr.shape, F32)

    wq = wq_scr[...]

    def update(keys, values, biases):
        s = [lax.dot_general(wq, k, (((1,), (1,)), ((), ())), preferred_element_type=F32) + b
             for k, b in zip(keys, biases)]
        m_old = m_scr[...]
        m_new = jnp.maximum(m_old, jnp.max(functools.reduce(jnp.maximum, s), axis=-1, keepdims=True))
        alpha = jnp.exp(m_old - m_new)
        p = [jnp.exp(x - m_new) for x in s]
        l_scr[...] = alpha * l_scr[...] + jnp.sum(functools.reduce(jnp.add, p), axis=-1, keepdims=True)
        pv = [jnp.dot(x.astype(BF16), v, preferred_element_type=F32) for x, v in zip(p, values)]
        acc_scr[...] = alpha * acc_scr[...] + functools.reduce(jnp.add, pv)
        m_scr[...] = m_new

    bias_far = bfar_ref[...]
    update([k_refs[g][...].reshape(rows, V_DIM).astype(BF16) for g in range(G)],
           [v_refs[g][...].reshape(rows, V_DIM).astype(BF16) for g in range(G)],
           [bias_far] * (G - 1) + [jnp.where(c == n_c - 1, blast_ref[...], bias_far)])

    @pl.when(c == n_c - 1)
    def _():
        update([kn_ref[...]], [vn_ref[...]], [bnew_ref[...]])
        o = acc_scr[...] / l_scr[...]
        half = o.shape[0] // 2
        lam = _diff_lambda(lv_ref, lam_init)
        d = o[:half] - lam * o[half:]
        o_ref[...] = _subln(d, sg_ref[...], lam_init).astype(o_ref.dtype)


def _attn_decode(layer, page_table, q, k_new, v_new, cache_k, cache_v, bias_far, bias_last, bias_new,
                 lam_vecs, subln_g, lam_init):
    G = PAGES_PER_STEP
    n_seq, n_pages = page_table.shape
    tok_heads = q.shape[1]
    rows = PAGE_SIZE * N_HEADS

    def page_spec(g):
        return pl.BlockSpec((None, None, PAGE_SIZE, N_HEADS, V_DIM),
                            lambda s, c, pt: (layer, pt[s, c * G + g], 0, 0, 0))

    def seq_spec(r):
        return pl.BlockSpec((None, r, V_DIM), lambda s, c, pt: (s, 0, 0))

    def const_spec(shape):
        return pl.BlockSpec(shape, lambda s, c, pt: (0,) * len(shape))

    grid_spec = pltpu.PrefetchScalarGridSpec(
        num_scalar_prefetch=1,
        grid=(n_seq, n_pages // G),
        in_specs=[seq_spec(tok_heads), seq_spec(tok_heads), seq_spec(tok_heads),
                  const_spec((2 * tok_heads, rows)), const_spec((2 * tok_heads, rows)),
                  const_spec((2 * tok_heads, tok_heads)),
                  const_spec((4, HEAD_DIM)), const_spec((1, V_DIM))]
                 + [page_spec(g) for g in range(G)] + [page_spec(g) for g in range(G)],
        out_specs=seq_spec(tok_heads),
        scratch_shapes=[pltpu.VMEM((2 * tok_heads, V_DIM), BF16),
                        pltpu.VMEM((2 * tok_heads, 1), F32), pltpu.VMEM((2 * tok_heads, 1), F32),
                        pltpu.VMEM((2 * tok_heads, V_DIM), F32)],
    )
    return pl.pallas_call(
        functools.partial(_attn_decode_kernel, lam_init=lam_init),
        grid_spec=grid_spec,
        out_shape=jax.ShapeDtypeStruct((n_seq, tok_heads, V_DIM), BF16),
        compiler_params=_params(2),
        name="attn_decode",
    )(page_table, q, k_new, v_new, bias_far, bias_last, bias_new, lam_vecs, subln_g,
      *([cache_k] * G), *([cache_v] * G))


def _gates(xc, wai_ref, ba_ref, bi_ref):
    xb = xc.astype(BF16)
    r_parts, i_parts = [], []
    for n in range(N_BLOCKS):
        sl = slice(n * BLOCK_W, (n + 1) * BLOCK_W)
        g = jnp.dot(xb[:, sl], wai_ref[n], preferred_element_type=F32)
        r_parts.append(g[:, :BLOCK_W])
        i_parts.append(g[:, BLOCK_W:])
    r = _sigmoid(jnp.concatenate(r_parts, axis=1) + ba_ref[...])
    i = _sigmoid(jnp.concatenate(i_parts, axis=1) + bi_ref[...])
    return r, i


def _lru_coeffs(xc, r, i, lam_ref):
    log_a = -RG_C * r * _softplus(-lam_ref[...])
    a = jnp.exp(log_a)
    u = jnp.sqrt(1.0 - a * a) * (i * xc)
    return a, u


def _merge_out(x, o, za, h, zr, ga, gr, wpa_ref, wpr_ref, wout_ref):
    branch_a = jnp.dot((o * _silu(za)).astype(BF16), wpa_ref[...], preferred_element_type=F32)
    branch_r = jnp.dot((h * _silu(zr)).astype(BF16), wpr_ref[...], preferred_element_type=F32)
    merged = _sigmoid(ga) * branch_a + _sigmoid(gr) * branch_r
    return x + jnp.dot(merged.astype(BF16), wout_ref[...], preferred_element_type=F32)


def _final_norm(y, g_ref):
    ms = jnp.mean(y * y, axis=-1, keepdims=True)
    return y * lax.rsqrt(ms + EPS) * g_ref[...]


def _post_prompt_kernel(x_ref, o_ref, za_ref, xr_ref, zr_ref, ga_ref, gr_ref,
                        cw_ref, cb_ref, wai_ref, ba_ref, bi_ref, lam_ref,
                        wpa_ref, wpr_ref, wout_ref, fg_ref,
                        y_ref, conv_ref, h_ref, xprev_scr, hprev_scr, *, final):
    T = POST_TILE
    t = pl.program_id(1)

    @pl.when(t == 0)
    def _():
        xprev_scr[...] = jnp.zeros(xprev_scr.shape, F32)
        hprev_scr[...] = jnp.zeros(hprev_scr.shape, F32)

    xr = xr_ref[...].astype(F32)
    row8 = lax.broadcasted_iota(jnp.int32, (8, D_RNN), 0)
    xprev = xprev_scr[...]

    def delayed(k):
        r = pltpu.roll(xr, k, 0)
        top = jnp.where(row8 < k, pltpu.roll(xprev, k, 0), r[0:8])
        return jnp.concatenate([top, r[8:]], axis=0)

    xc = cb_ref[...] + cw_ref[3:4, :] * xr
    for k in range(1, CONV_W):
        xc = xc + cw_ref[CONV_W - 1 - k:CONV_W - k, :] * delayed(k)

    r, i = _gates(xc, wai_ref, ba_ref, bi_ref)
    a, u = _lru_coeffs(xc, r, i, lam_ref)

    row = lax.broadcasted_iota(jnp.int32, (T, D_RNN), 0)
    u = jnp.where(row == 0, u + a * hprev_scr[7:8, :], u)
    d = 1
    while d < T:
        keep = row >= d
        u = jnp.where(keep, a * pltpu.roll(u, d, 0) + u, u)
        a = jnp.where(keep, a * pltpu.roll(a, d, 0), a)
        d *= 2
    h = u

    xprev_scr[...] = xr[T - 8:, :]
    hprev_scr[...] = h[T - 8:, :]
    conv_ref[...] = xr[T - 8:, :]
    h_ref[...] = h[T - 8:, :]

    y = _merge_out(x_ref[...], o_ref[...].astype(F32), za_ref[...].astype(F32), h,
                   zr_ref[...].astype(F32), ga_ref[...].astype(F32), gr_ref[...].astype(F32),
                   wpa_ref, wpr_ref, wout_ref)
    if final:
        y = _final_norm(y, fg_ref)
    y_ref[...] = y


def _post_prompt(x, o, p, weights, final, batch, seq):
    T = POST_TILE
    nt = seq // T

    def col_spec(col):
        return pl.BlockSpec((T, D_MODEL), lambda b, t: (b * nt + t, col))

    def const_spec(shape):
        return pl.BlockSpec(shape, lambda b, t: (0,) * len(shape))

    tail_spec = pl.BlockSpec((None, 8, D_RNN), lambda b, t: (b, 0, 0))
    (cw, cb, wai, ba, bi, lam, wpa, wpr, wout, fg) = weights
    return pl.pallas_call(
        functools.partial(_post_prompt_kernel, final=final),
        grid=(batch, nt),
        in_specs=[col_spec(0), col_spec(0), col_spec(COL_ZA), col_spec(COL_XR), col_spec(COL_ZR),
                  col_spec(COL_GA), col_spec(COL_GR),
                  const_spec(cw.shape), const_spec(cb.shape), const_spec(wai.shape),
                  const_spec(ba.shape), const_spec(bi.shape), const_spec(lam.shape),
                  const_spec(wpa.shape), const_spec(wpr.shape), const_spec(wout.shape),
                  const_spec(fg.shape)],
        out_specs=[col_spec(0), tail_spec, tail_spec],
        out_shape=[jax.ShapeDtypeStruct((batch * seq, D_MODEL), F32),
                   jax.ShapeDtypeStruct((batch, 8, D_RNN), F32),
                   jax.ShapeDtypeStruct((batch, 8, D_RNN), F32)],
        scratch_shapes=[pltpu.VMEM((8, D_RNN), F32), pltpu.VMEM((8, D_RNN), F32)],
        compiler_params=_params(2),
        name="post_prompt",
    )(x, o, p, p, p, p, p, cw, cb, wai, ba, bi, lam, wpa, wpr, wout, fg)


def _post_sample_kernel(x_ref, o_ref, za_ref, xr_ref, zr_ref, ga_ref, gr_ref, buf_ref, h0_ref,
                        cw_ref, cb_ref, wai_ref, ba_ref, bi_ref, lam_ref,
                        wpa_ref, wpr_ref, wout_ref, fg_ref,
                        y_ref, conv_ref, h_ref, *, final, n_seq, n_tok):
    xr = xr_ref[...].astype(F32)
    hist = [buf_ref[k] for k in range(CONV_W - 1)]
    hist += [xr[t * n_seq:(t + 1) * n_seq, :] for t in range(n_tok)]
    xc_parts = []
    for t in range(n_tok):
        acc = cb_ref[...]
        for j in range(CONV_W):
            acc = acc + cw_ref[j:j + 1, :] * hist[t + j]
        xc_parts.append(acc)
    xc = jnp.concatenate(xc_parts, axis=0)
    for k in range(CONV_W - 1):
        conv_ref[k] = hist[n_tok + k]

    r, i = _gates(xc, wai_ref, ba_ref, bi_ref)
    a, u = _lru_coeffs(xc, r, i, lam_ref)
    h = h0_ref[...]
    h_parts = []
    for t in range(n_tok):
        sl = slice(t * n_seq, (t + 1) * n_seq)
        h = a[sl, :] * h + u[sl, :]
        h_parts.append(h)
    h_ref[...] = h
    h_all = jnp.concatenate(h_parts, axis=0)

    y = _merge_out(x_ref[...], o_ref[...].astype(F32), za_ref[...].astype(F32), h_all,
                   zr_ref[...].astype(F32), ga_ref[...].astype(F32), gr_ref[...].astype(F32),
                   wpa_ref, wpr_ref, wout_ref)
    if final:
        y = _final_norm(y, fg_ref)
    y_ref[...] = y


def _post_sample(x, o, p, buf, h0, weights, final, n_seq, n_tok):
    n = n_seq * n_tok

    def col_spec(col):
        return pl.BlockSpec((n, D_MODEL), lambda i: (0, col))

    def const_spec(shape):
        return pl.BlockSpec(shape, lambda i: (0,) * len(shape))

    (cw, cb, wai, ba, bi, lam, wpa, wpr, wout, fg) = weights
    return pl.pallas_call(
        functools.partial(_post_sample_kernel, final=final, n_seq=n_seq, n_tok=n_tok),
        grid=(1,),
        in_specs=[col_spec(0), col_spec(0), col_spec(COL_ZA), col_spec(COL_XR), col_spec(COL_ZR),
                  col_spec(COL_GA), col_spec(COL_GR), const_spec(buf.shape), const_spec(h0.shape),
                  const_spec(cw.shape), const_spec(cb.shape), const_spec(wai.shape),
                  const_spec(ba.shape), const_spec(bi.shape), const_spec(lam.shape),
                  const_spec(wpa.shape), const_spec(wpr.shape), const_spec(wout.shape),
                  const_spec(fg.shape)],
        out_specs=[col_spec(0), const_spec(buf.shape), const_spec(h0.shape)],
        out_shape=[jax.ShapeDtypeStruct((n, D_MODEL), F32),
                   jax.ShapeDtypeStruct(buf.shape, F32),
                   jax.ShapeDtypeStruct(h0.shape, F32)],
        compiler_params=_params(1),
        name="post_sample",
    )(x, o, p, p, p, p, p, buf, h0, cw, cb, wai, ba, bi, lam, wpa, wpr, wout, fg)


def _t5_bucket(dist):
    n = jnp.maximum(dist, 0)
    max_exact = N_BUCKETS // 2
    nf = jnp.maximum(n, 1).astype(F32)
    large = max_exact + (jnp.log(nf / max_exact) / math.log(MAX_DIST / max_exact)
                         * (N_BUCKETS - max_exact)).astype(jnp.int32)
    large = jnp.minimum(large, N_BUCKETS - 1)
    return jnp.where(n < max_exact, n, large)


def _bias_lookup(rel_bias, dist):
    return rel_bias[_t5_bucket(dist)].astype(F32)


def _toeplitz(v, rows):
    n_heads, length = v.shape
    return jnp.tile(v, (1, rows))[:, :rows * (length - 1)].reshape(n_heads, rows, length - 1)


def _prompt_bias(rel_bias):
    TQ, TK = ATT_TQ, ATT_TK
    length = TQ + TK
    k = jnp.arange(length)
    rel = jnp.where(k < TQ, k, k - length)
    tiles = []
    for s in range(1 + TQ // TK):
        dist = TK * (1 - s) + rel
        v = jnp.where(dist[:, None] >= 0, _bias_lookup(rel_bias, dist), MASKED).T
        tiles.append(_toeplitz(v, TK)[:, :, :TQ])
    far = rel_bias[N_BUCKETS - 1].astype(F32)
    return far, jnp.stack(tiles, axis=1)


def _decode_bias(rel_bias, n_tok):
    def expand(b, valid):
        n_keys = b.shape[1]
        b = jnp.where(valid[:, :, None], b, MASKED)
        b = jnp.transpose(b, (0, 2, 1)).reshape(n_tok * N_HEADS, n_keys)
        b = jnp.repeat(b, N_HEADS, axis=1)
        hq = jnp.arange(n_tok * N_HEADS)[:, None] % N_HEADS
        hk = jnp.arange(n_keys * N_HEADS)[None, :] % N_HEADS
        b = jnp.where(hq == hk, b, MASKED)
        return jnp.concatenate([b, b], axis=0)

    i = jnp.arange(n_tok)[:, None]
    d_last = PAGE_SIZE + i - jnp.arange(PAGE_SIZE)[None, :]
    d_new = i - jnp.arange(n_tok)[None, :]
    far = expand(jnp.broadcast_to(rel_bias[N_BUCKETS - 1].astype(F32), (n_tok, PAGE_SIZE, N_HEADS)),
                 jnp.ones((n_tok, PAGE_SIZE), bool))
    last = expand(_bias_lookup(rel_bias, d_last), d_last >= 0)
    new = expand(_bias_lookup(rel_bias, d_new), d_new >= 0)
    return far, last, new


def kernel(x_prompt, x_sample, cache_k, cache_v, state_conv, state_h, page_table, rel_bias,
           norm_g, w_in, lam_vecs, subln_g, conv_w, conv_b, w_a, b_a, w_i, b_i, lru_lam,
           w_pa, w_pr, w_out, final_norm_g):
    batch, seq, _ = x_prompt.shape
    n_seq, n_tok, _ = x_sample.shape
    depth = w_in.shape[0]
    assert seq % ATT_TQ == 0 and ATT_TQ % ATT_TK == 0
    assert seq % POST_TILE == 0 and (batch * seq) % PROJ_TILE == 0
    assert page_table.shape[1] % PAGES_PER_STEP == 0 and cache_k.shape[2] == PAGE_SIZE
    assert ATT_TK >= MAX_DIST and PAGE_SIZE >= MAX_DIST

    cfar, bias_near = _prompt_bias(rel_bias)
    dec_far, dec_last, dec_new = _decode_bias(rel_bias, n_tok)

    yp = x_prompt.reshape(batch * seq, D_MODEL)
    ys = jnp.swapaxes(x_sample, 0, 1).reshape(n_tok * n_seq, D_MODEL)
    row = lambda v: v.reshape(1, -1)
    outs = [[] for _ in range(8)]

    for l in range(depth):
        lam_init = 0.8 - 0.6 * math.exp(-0.3 * l)
        final = l == depth - 1
        w_in_bf = w_in[l].astype(BF16)
        wai = jnp.concatenate([w_a[l], w_i[l]], axis=-1).astype(BF16)
        weights = (conv_w[l], row(conv_b[l]), wai, row(b_a[l]), row(b_i[l]), row(lru_lam[l]),
                   w_pa[l].astype(BF16), w_pr[l].astype(BF16), w_out[l].astype(BF16),
                   row(final_norm_g))
        g_in, sg = row(norm_g[l]), row(subln_g[l])

        p, kf, vf = _inproj(yp, g_in, w_in_bf, PROJ_TILE)
        o = _attn_prompt(p, cfar, bias_near, lam_vecs[l], sg, lam_init, batch, seq)
        yp, conv_t, h_t = _post_prompt(yp, o, p, weights, final, batch, seq)
        outs[0].append(kf.reshape(batch, seq, N_HEADS, V_DIM))
        outs[1].append(vf.reshape(batch, seq, N_HEADS, V_DIM))
        outs[2].append(conv_t[:, 8 - (CONV_W - 1):, :])
        outs[3].append(h_t[:, 7, :])

        ps, kfs, vfs = _inproj(ys, g_in, w_in_bf, n_tok * n_seq)
        seq_major = lambda v: jnp.swapaxes(v.reshape(n_tok, n_seq, N_HEADS, V_DIM), 0, 1)
        q_s = seq_major(ps[:, :D_MODEL]).reshape(n_seq, n_tok * N_HEADS, V_DIM)
        k_s, v_s = seq_major(kfs), seq_major(vfs)
        o_s = _attn_decode(l, page_table, q_s,
                           k_s.reshape(n_seq, n_tok * N_HEADS, V_DIM).astype(BF16),
                           v_s.reshape(n_seq, n_tok * N_HEADS, V_DIM).astype(BF16),
                           cache_k, cache_v, dec_far, dec_last, dec_new, lam_vecs[l], sg, lam_init)
        o_s = jnp.swapaxes(o_s.reshape(n_seq, n_tok, D_MODEL), 0, 1).reshape(n_tok * n_seq, D_MODEL)
        ys, conv_s, h_s = _post_sample(ys, o_s, ps, jnp.swapaxes(state_conv[l], 0, 1), state_h[l],
                                       weights, final, n_seq, n_tok)
        outs[4].append(k_s)
        outs[5].append(v_s)
        outs[6].append(jnp.swapaxes(conv_s, 0, 1))
        outs[7].append(h_s)

    y_prompt = yp.reshape(batch, seq, D_MODEL)
    y_sample = jnp.swapaxes(ys.reshape(n_tok, n_seq, D_MODEL), 0, 1)
    return (y_prompt, y_sample) + tuple(jnp.stack(o) for o in outs)
```

```python
import functools
import math

import jax
import jax.numpy as jnp
from jax import lax
from jax.experimental import pallas as pl
from jax.experimental.pallas import tpu as pltpu

F32 = jnp.float32
BF16 = jnp.bfloat16

D_MODEL = 1024
N_HEADS = 8
HEAD_DIM = 64
V_DIM = 2 * HEAD_DIM
D_RNN = 1024
N_BLOCKS = 8
BLOCK_W = D_RNN // N_BLOCKS
CONV_W = 4
RG_C = 8.0
N_BUCKETS = 32
MAX_DIST = 128
PAGE_SIZE = 128
EPS = 1e-6
ATT_SCALE = HEAD_DIM ** -0.5
N_PROJ = 8
COL_Q, COL_K, COL_V, COL_ZA, COL_XR, COL_ZR, COL_GA, COL_GR = range(N_PROJ)
MASKED = -1e30

ATT_TQ = 1024
ATT_TK = 256
ONES_ROWS = 16
SUBLANES = 8
POST_TILE = 256
PROJ_TILE = 1024
PAGES_PER_STEP = 16
DECODE_GROUPS = 1
VMEM_LIMIT = 56 * 1024 * 1024


def _params(n_axes):
    return pltpu.CompilerParams(dimension_semantics=("arbitrary",) * n_axes,
                                vmem_limit_bytes=VMEM_LIMIT)


def _sigmoid(x):
    return 1.0 / (1.0 + jnp.exp(-x))


def _silu(x):
    return x * _sigmoid(x)


def _softplus(x):
    return jnp.maximum(x, 0.0) + jnp.log1p(jnp.exp(-jnp.abs(x)))


def _diff_lambda(lv_ref, lam_init):
    a = jnp.sum(lv_ref[0:1, :] * lv_ref[1:2, :], axis=1, keepdims=True)
    b = jnp.sum(lv_ref[2:3, :] * lv_ref[3:4, :], axis=1, keepdims=True)
    return jnp.exp(a) - jnp.exp(b) + lam_init


def _split_maps(q):
    lane = lax.broadcasted_iota(jnp.int32, q.shape, 1)
    qs = q * jnp.asarray(ATT_SCALE, q.dtype)
    zero = jnp.zeros_like(qs)
    return jnp.concatenate([jnp.where(lane < HEAD_DIM, qs, zero),
                            jnp.where(lane >= HEAD_DIM, qs, zero)], axis=0)


def _subln(d, g, lam_init):
    ms = jnp.mean(d * d, axis=-1, keepdims=True)
    return d * lax.rsqrt(ms + EPS) * g * (1.0 - lam_init)


def _inproj_kernel(x_ref, g_ref, w_ref, p_ref, kf_ref, vf_ref, u_scr):
    j = pl.program_id(1)

    @pl.when(j == 0)
    def _():
        x = x_ref[...]
        ms = jnp.mean(x * x, axis=-1, keepdims=True)
        u_scr[...] = (x * lax.rsqrt(ms + EPS) * g_ref[...]).astype(BF16)

    acc = jnp.dot(u_scr[...], w_ref[...], preferred_element_type=F32)
    p_ref[...] = acc.astype(BF16)

    @pl.when(j == COL_K)
    def _():
        kf_ref[...] = acc

    @pl.when(j == COL_V)
    def _():
        vf_ref[...] = acc


def _inproj(x, g, w_bf, tile):
    n = x.shape[0]
    return pl.pallas_call(
        _inproj_kernel,
        grid=(n // tile, N_PROJ),
        in_specs=[
            pl.BlockSpec((tile, D_MODEL), lambda i, j: (i, 0)),
            pl.BlockSpec((1, D_MODEL), lambda i, j: (0, 0)),
            pl.BlockSpec((D_MODEL, D_MODEL), lambda i, j: (0, j)),
        ],
        out_specs=[
            pl.BlockSpec((tile, D_MODEL), lambda i, j: (i, j)),
            pl.BlockSpec((tile, D_MODEL), lambda i, j: (i, 0)),
            pl.BlockSpec((tile, D_MODEL), lambda i, j: (i, 0)),
        ],
        out_shape=[
            jax.ShapeDtypeStruct((n, N_PROJ * D_MODEL), BF16),
            jax.ShapeDtypeStruct((n, D_MODEL), F32),
            jax.ShapeDtypeStruct((n, D_MODEL), F32),
        ],
        scratch_shapes=[pltpu.VMEM((tile, D_MODEL), BF16)],
        compiler_params=_params(2),
        name="inproj",
    )(x, g, w_bf)


def _attn_prompt_kernel(cfar_ref, q_ref, k_ref, v_ref, bias_ref, lv_ref, sg_ref,
                        o_ref, vte_scr, m_scr, acc_scr, *, lam_init, seq):
    TQ, TK = ATT_TQ, ATT_TK
    ratio = TQ // TK

    for j in range(seq // TK):
        vt = v_ref[j * TK:(j + 1) * TK, :].astype(F32).T
        vte_scr[j, 0:V_DIM, :] = vt.astype(BF16)
        vte_scr[j, V_DIM:, :] = jnp.ones((ONES_ROWS, TK), BF16)

    lam = _diff_lambda(lv_ref, lam_init)
    c_far = cfar_ref[...]

    def q_tile(qi, carry):
        q_start = pl.multiple_of(qi * TQ, TQ)
        qq = _split_maps(q_ref[pl.ds(q_start, TQ), :])
        m_scr[...] = jnp.full(m_scr.shape, MASKED, F32)
        acc_scr[...] = jnp.zeros(acc_scr.shape, F32)

        def step(kj, bias_tile, lo):
            w = TQ - lo
            kt = k_ref[pl.ds(pl.multiple_of(kj * TK, TK), TK), :]
            qs = jnp.concatenate([qq[lo:TQ], qq[TQ + lo:]], axis=0)
            st = lax.dot_general(kt, qs, (((1,), (1,)), ((), ())), preferred_element_type=F32)
            m_old = jnp.concatenate([m_scr[0, :, lo:], m_scr[1, :, lo:]], axis=1)
            if bias_tile is None:
                m_new = jnp.maximum(m_old, jnp.max(st, axis=0, keepdims=True) + c_far)
                shift = m_new - c_far
            else:
                b = bias_tile[:, :w]
                st = st + jnp.concatenate([b, b], axis=1)
                m_new = jnp.maximum(m_old, jnp.max(st, axis=0, keepdims=True))
                shift = m_new
            alpha = jnp.exp(m_old - m_new)
            p = jnp.exp(st - shift).astype(BF16)
            pv = jnp.dot(vte_scr[kj], p, preferred_element_type=F32)
            for mp in range(2):
                cols = slice(mp * w, (mp + 1) * w)
                acc_scr[mp, :, lo:] = alpha[:, cols] * acc_scr[mp, :, lo:] + pv[:, cols]
                m_scr[mp, :, lo:] = m_new[:, cols]

        def far_body(kj, c):
            step(kj, None, 0)
            return c

        first_near = qi * ratio - 1
        lax.fori_loop(0, first_near, far_body, 0)

        @pl.when(qi >= 1)
        def _():
            step(first_near, bias_ref[0], 0)

        for s in range(ratio):
            step(qi * ratio + s, bias_ref[1], s * TK)

        o = [acc_scr[mp, 0:V_DIM, :] / acc_scr[mp, V_DIM:V_DIM + 1, :] for mp in range(2)]
        d = (o[0] - lam * o[1]).T
        o_ref[pl.ds(q_start, TQ), :] = _subln(d, sg_ref[...], lam_init).astype(o_ref.dtype)
        return carry

    lax.fori_loop(0, seq // TQ, q_tile, 0)


def _attn_prompt(p, cfar, bias_near, lam_vecs, subln_g, lam_init, batch, seq):
    TQ, TK = ATT_TQ, ATT_TK
    n_near = bias_near.shape[1]
    return pl.pallas_call(
        functools.partial(_attn_prompt_kernel, lam_init=lam_init, seq=seq),
        grid=(N_HEADS, batch),
        in_specs=[
            pl.BlockSpec((None, 1, 1), lambda h, b: (h, 0, 0)),
            pl.BlockSpec((seq, V_DIM), lambda h, b: (b, COL_Q * N_HEADS + h)),
            pl.BlockSpec((seq, V_DIM), lambda h, b: (b, COL_K * N_HEADS + h)),
            pl.BlockSpec((seq, V_DIM), lambda h, b: (b, COL_V * N_HEADS + h)),
            pl.BlockSpec((None, n_near, TK, TQ), lambda h, b: (h, 0, 0, 0)),
            pl.BlockSpec((4, HEAD_DIM), lambda h, b: (0, 0)),
            pl.BlockSpec((1, V_DIM), lambda h, b: (0, 0)),
        ],
        out_specs=pl.BlockSpec((seq, V_DIM), lambda h, b: (b, h)),
        out_shape=jax.ShapeDtypeStruct((batch * seq, N_HEADS * V_DIM), BF16),
        scratch_shapes=[pltpu.VMEM((seq // TK, V_DIM + ONES_ROWS, TK), BF16),
                        pltpu.VMEM((2, 1, TQ), F32),
                        pltpu.VMEM((2, V_DIM + ONES_ROWS, TQ), F32)],
        compiler_params=_params(2),
        name="attn_prompt",
    )(cfar.reshape(N_HEADS, 1, 1), p, p, p, bias_near, lam_vecs, subln_g)


def _attn_decode_kernel(pt_ref, q_ref, kn_ref, vn_ref, bfar_ref, blast_ref, bnew_ref, lv_ref, sg_ref,
                        *rest, lam_init):
    G = PAGES_PER_STEP
    k_refs = rest[:G]
    v_refs = rest[G:2 * G]
    o_ref = rest[2 * G]
    wq_scr, m_scr, l_scr, acc_scr = rest[2 * G + 1:]
    c = pl.program_id(1)
    n_c = pl.num_programs(1)
    rows = PAGE_SIZE * N_HEADS

    @pl.when(c == 0)
    def _():
        wq_scr[...] = _split_maps(q_ref[...])
        m_scr[...] = jnp.full(m_scr.shape, MASKED, F32)
        l_scr[...] = jnp.zeros(l_scr.shape, F32)
        acc_scr[...] = jnp.zeros(acc_scr.shape, F32)

    wq = wq_scr[...]

    def update(keys, values, biases):
        n_groups = min(DECODE_GROUPS, len(keys))
        size = len(keys) // n_groups
        parts = []
        for g in range(n_groups):
            blk = slice(g * size, (g + 1) * size)
            s = [lax.dot_general(wq, k, (((1,), (1,)), ((), ())), preferred_element_type=F32) + b
                 for k, b in zip(keys[blk], biases[blk])]
            m_grp = jnp.max(functools.reduce(jnp.maximum, s), axis=-1, keepdims=True)
            p = [jnp.exp(x - m_grp) for x in s]
            l_grp = jnp.sum(functools.reduce(jnp.add, p), axis=-1, keepdims=True)
            pv = [jnp.dot(x.astype(BF16), v, preferred_element_type=F32) for x, v in zip(p, values[blk])]
            parts.append((m_grp, l_grp, functools.reduce(jnp.add, pv)))
        m_old = m_scr[...]
        m_new = functools.reduce(jnp.maximum, [m_grp for m_grp, _, _ in parts], m_old)
        alpha = jnp.exp(m_old - m_new)
        l_sum = alpha * l_scr[...]
        acc = alpha * acc_scr[...]
        for m_grp, l_grp, a_grp in parts:
            w = jnp.exp(m_grp - m_new)
            l_sum = l_sum + w * l_grp
            acc = acc + w * a_grp
        l_scr[...] = l_sum
        acc_scr[...] = acc
        m_scr[...] = m_new

    bias_far = bfar_ref[...]
    update([k_refs[g][...].reshape(rows, V_DIM).astype(BF16) for g in range(G)],
           [v_refs[g][...].reshape(rows, V_DIM).astype(BF16) for g in range(G)],
           [bias_far] * (G - 1) + [jnp.where(c == n_c - 1, blast_ref[...], bias_far)])

    @pl.when(c == n_c - 1)
    def _():
        update([kn_ref[...]], [vn_ref[...]], [bnew_ref[...]])
        o = acc_scr[...] / l_scr[...]
        half = o.shape[0] // 2
        lam = _diff_lambda(lv_ref, lam_init)
        d = o[:half] - lam * o[half:]
        o_ref[...] = _subln(d, sg_ref[...], lam_init).astype(o_ref.dtype)


def _attn_decode(layer, page_table, q, k_new, v_new, cache_k, cache_v, bias_far, bias_last, bias_new,
                 lam_vecs, subln_g, lam_init):
    G = PAGES_PER_STEP
    n_seq, n_pages = page_table.shape
    tok_heads = q.shape[1]
    rows = PAGE_SIZE * N_HEADS

    def page_spec(g):
        return pl.BlockSpec((None, None, PAGE_SIZE, N_HEADS, V_DIM),
                            lambda s, c, pt: (layer, pt[s, c * G + g], 0, 0, 0))

    def seq_spec(r):
        return pl.BlockSpec((None, r, V_DIM), lambda s, c, pt: (s, 0, 0))

    def const_spec(shape):
        return pl.BlockSpec(shape, lambda s, c, pt: (0,) * len(shape))

    grid_spec = pltpu.PrefetchScalarGridSpec(
        num_scalar_prefetch=1,
        grid=(n_seq, n_pages // G),
        in_specs=[seq_spec(tok_heads), seq_spec(tok_heads), seq_spec(tok_heads),
                  const_spec((2 * tok_heads, rows)), const_spec((2 * tok_heads, rows)),
                  const_spec((2 * tok_heads, tok_heads)),
                  const_spec((4, HEAD_DIM)), const_spec((1, V_DIM))]
                 + [page_spec(g) for g in range(G)] + [page_spec(g) for g in range(G)],
        out_specs=seq_spec(tok_heads),
        scratch_shapes=[pltpu.VMEM((2 * tok_heads, V_DIM), BF16),
                        pltpu.VMEM((2 * tok_heads, 1), F32), pltpu.VMEM((2 * tok_heads, 1), F32),
                        pltpu.VMEM((2 * tok_heads, V_DIM), F32)],
    )
    return pl.pallas_call(
        functools.partial(_attn_decode_kernel, lam_init=lam_init),
        grid_spec=grid_spec,
        out_shape=jax.ShapeDtypeStruct((n_seq, tok_heads, V_DIM), BF16),
        compiler_params=_params(2),
        name="attn_decode",
    )(page_table, q, k_new, v_new, bias_far, bias_last, bias_new, lam_vecs, subln_g,
      *([cache_k] * G), *([cache_v] * G))


def _gates(xc, wai_ref, ba_ref, bi_ref):
    xb = xc.astype(BF16)
    r_parts, i_parts = [], []
    for n in range(N_BLOCKS):
        sl = slice(n * BLOCK_W, (n + 1) * BLOCK_W)
        g = jnp.dot(xb[:, sl], wai_ref[n], preferred_element_type=F32)
        r_parts.append(g[:, :BLOCK_W])
        i_parts.append(g[:, BLOCK_W:])
    r = _sigmoid(jnp.concatenate(r_parts, axis=1) + ba_ref[...])
    i = _sigmoid(jnp.concatenate(i_parts, axis=1) + bi_ref[...])
    return r, i


def _lru_coeffs(xc, r, i, lam_ref):
    log_a = -RG_C * r * _softplus(-lam_ref[...])
    a = jnp.exp(log_a)
    u = jnp.sqrt(1.0 - a * a) * (i * xc)
    return a, u


def _merge_out(x, o, za, h, zr, ga, gr, wpa_ref, wpr_ref, wout_ref):
    branch_a = jnp.dot((o * _silu(za)).astype(BF16), wpa_ref[...], preferred_element_type=F32)
    branch_r = jnp.dot((h * _silu(zr)).astype(BF16), wpr_ref[...], preferred_element_type=F32)
    merged = _sigmoid(ga) * branch_a + _sigmoid(gr) * branch_r
    return x + jnp.dot(merged.astype(BF16), wout_ref[...], preferred_element_type=F32)


def _final_norm(y, g_ref):
    ms = jnp.mean(y * y, axis=-1, keepdims=True)
    return y * lax.rsqrt(ms + EPS) * g_ref[...]


def _post_prompt_kernel(x_ref, o_ref, za_ref, xr_ref, zr_ref, ga_ref, gr_ref,
                        cw_ref, cb_ref, wai_ref, ba_ref, bi_ref, lam_ref,
                        wpa_ref, wpr_ref, wout_ref, fg_ref,
                        y_ref, conv_ref, h_ref, xprev_scr, hprev_scr, *, final):
    T = POST_TILE
    t = pl.program_id(1)

    @pl.when(t == 0)
    def _():
        xprev_scr[...] = jnp.zeros(xprev_scr.shape, F32)
        hprev_scr[...] = jnp.zeros(hprev_scr.shape, F32)

    xr = xr_ref[...].astype(F32)
    row8 = lax.broadcasted_iota(jnp.int32, (8, D_RNN), 0)
    xprev = xprev_scr[...]

    def delayed(k):
        r = pltpu.roll(xr, k, 0)
        top = jnp.where(row8 < k, pltpu.roll(xprev, k, 0), r[0:8])
        return jnp.concatenate([top, r[8:]], axis=0)

    xc = cb_ref[...] + cw_ref[3:4, :] * xr
    for k in range(1, CONV_W):
        xc = xc + cw_ref[CONV_W - 1 - k:CONV_W - k, :] * delayed(k)

    r, i = _gates(xc, wai_ref, ba_ref, bi_ref)
    a, u = _lru_coeffs(xc, r, i, lam_ref)

    sub = lax.broadcasted_iota(jnp.int32, (T, D_RNN), 0) % SUBLANES
    d = 1
    while d < SUBLANES:
        keep = sub >= d
        u = jnp.where(keep, a * pltpu.roll(u, d, 0) + u, u)
        a = jnp.where(keep, a * pltpu.roll(a, d, 0), a)
        d *= 2
    carry = hprev_scr[SUBLANES - 1:SUBLANES, :]
    groups = []
    for g in range(T // SUBLANES):
        rows = slice(g * SUBLANES, (g + 1) * SUBLANES)
        hg = u[rows] + a[rows] * carry
        groups.append(hg)
        carry = hg[SUBLANES - 1:SUBLANES, :]
    h = jnp.concatenate(groups, axis=0)

    xprev_scr[...] = xr[T - 8:, :]
    hprev_scr[...] = h[T - 8:, :]
    conv_ref[...] = xr[T - 8:, :]
    h_ref[...] = h[T - 8:, :]

    y = _merge_out(x_ref[...], o_ref[...].astype(F32), za_ref[...].astype(F32), h,
                   zr_ref[...].astype(F32), ga_ref[...].astype(F32), gr_ref[...].astype(F32),
                   wpa_ref, wpr_ref, wout_ref)
    if final:
        y = _final_norm(y, fg_ref)
    y_ref[...] = y


def _post_prompt(x, o, p, weights, final, batch, seq):
    T = POST_TILE
    nt = seq // T

    def col_spec(col):
        return pl.BlockSpec((T, D_MODEL), lambda b, t: (b * nt + t, col))

    def const_spec(shape):
        return pl.BlockSpec(shape, lambda b, t: (0,) * len(shape))

    tail_spec = pl.BlockSpec((None, 8, D_RNN), lambda b, t: (b, 0, 0))
    (cw, cb, wai, ba, bi, lam, wpa, wpr, wout, fg) = weights
    return pl.pallas_call(
        functools.partial(_post_prompt_kernel, final=final),
        grid=(batch, nt),
        in_specs=[col_spec(0), col_spec(0), col_spec(COL_ZA), col_spec(COL_XR), col_spec(COL_ZR),
                  col_spec(COL_GA), col_spec(COL_GR),
                  const_spec(cw.shape), const_spec(cb.shape), const_spec(wai.shape),
                  const_spec(ba.shape), const_spec(bi.shape), const_spec(lam.shape),
                  const_spec(wpa.shape), const_spec(wpr.shape), const_spec(wout.shape),
                  const_spec(fg.shape)],
        out_specs=[col_spec(0), tail_spec, tail_spec],
        out_shape=[jax.ShapeDtypeStruct((batch * seq, D_MODEL), F32),
                   jax.ShapeDtypeStruct((batch, 8, D_RNN), F32),
                   jax.ShapeDtypeStruct((batch, 8, D_RNN), F32)],
        scratch_shapes=[pltpu.VMEM((8, D_RNN), F32), pltpu.VMEM((8, D_RNN), F32)],
        compiler_params=_params(2),
        name="post_prompt",
    )(x, o, p, p, p, p, p, cw, cb, wai, ba, bi, lam, wpa, wpr, wout, fg)


def _post_sample_kernel(x_ref, o_ref, za_ref, xr_ref, zr_ref, ga_ref, gr_ref, buf_ref, h0_ref,
                        cw_ref, cb_ref, wai_ref, ba_ref, bi_ref, lam_ref,
                        wpa_ref, wpr_ref, wout_ref, fg_ref,
                        y_ref, conv_ref, h_ref, *, final, n_seq, n_tok):
    xr = xr_ref[...].astype(F32)
    hist = [buf_ref[k] for k in range(CONV_W - 1)]
    hist += [xr[t * n_seq:(t + 1) * n_seq, :] for t in range(n_tok)]
    xc_parts = []
    for t in range(n_tok):
        acc = cb_ref[...]
        for j in range(CONV_W):
            acc = acc + cw_ref[j:j + 1, :] * hist[t + j]
        xc_parts.append(acc)
    xc = jnp.concatenate(xc_parts, axis=0)
    for k in range(CONV_W - 1):
        conv_ref[k] = hist[n_tok + k]

    r, i = _gates(xc, wai_ref, ba_ref, bi_ref)
    a, u = _lru_coeffs(xc, r, i, lam_ref)
    h = h0_ref[...]
    h_parts = []
    for t in range(n_tok):
        sl = slice(t * n_seq, (t + 1) * n_seq)
        h = a[sl, :] * h + u[sl, :]
        h_parts.append(h)
    h_ref[...] = h
    h_all = jnp.concatenate(h_parts, axis=0)

    y = _merge_out(x_ref[...], o_ref[...].astype(F32), za_ref[...].astype(F32), h_all,
                   zr_ref[...].astype(F32), ga_ref[...].astype(F32), gr_ref[...].astype(F32),
                   wpa_ref, wpr_ref, wout_ref)
    if final:
        y = _final_norm(y, fg_ref)
    y_ref[...] = y


def _post_sample(x, o, p, buf, h0, weights, final, n_seq, n_tok):
    n = n_seq * n_tok

    def col_spec(col):
        return pl.BlockSpec((n, D_MODEL), lambda i: (0, col))

    def const_spec(shape):
        return pl.BlockSpec(shape, lambda i: (0,) * len(shape))

    (cw, cb, wai, ba, bi, lam, wpa, wpr, wout, fg) = weights
    return pl.pallas_call(
        functools.partial(_post_sample_kernel, final=final, n_seq=n_seq, n_tok=n_tok),
        grid=(1,),
        in_specs=[col_spec(0), col_spec(0), col_spec(COL_ZA), col_spec(COL_XR), col_spec(COL_ZR),
                  col_spec(COL_GA), col_spec(COL_GR), const_spec(buf.shape), const_spec(h0.shape),
                  const_spec(cw.shape), const_spec(cb.shape), const_spec(wai.shape),
                  const_spec(ba.shape), const_spec(bi.shape), const_spec(lam.shape),
                  const_spec(wpa.shape), const_spec(wpr.shape), const_spec(wout.shape),
                  const_spec(fg.shape)],
        out_specs=[col_spec(0), const_spec(buf.shape), const_spec(h0.shape)],
        out_shape=[jax.ShapeDtypeStruct((n, D_MODEL), F32),
                   jax.ShapeDtypeStruct(buf.shape, F32),
                   jax.ShapeDtypeStruct(h0.shape, F32)],
        compiler_params=_params(1),
        name="post_sample",
    )(x, o, p, p, p, p, p, buf, h0, cw, cb, wai, ba, bi, lam, wpa, wpr, wout, fg)


def _t5_bucket(dist):
    n = jnp.maximum(dist, 0)
    max_exact = N_BUCKETS // 2
    nf = jnp.maximum(n, 1).astype(F32)
    large = max_exact + (jnp.log(nf / max_exact) / math.log(MAX_DIST / max_exact)
                         * (N_BUCKETS - max_exact)).astype(jnp.int32)
    large = jnp.minimum(large, N_BUCKETS - 1)
    return jnp.where(n < max_exact, n, large)


def _bias_lookup(rel_bias, dist):
    return rel_bias[_t5_bucket(dist)].astype(F32)


def _toeplitz(v, rows):
    n_heads, length = v.shape
    return jnp.tile(v, (1, rows))[:, :rows * (length - 1)].reshape(n_heads, rows, length - 1)


def _prompt_bias(rel_bias):
    TQ, TK = ATT_TQ, ATT_TK
    length = TQ + TK
    k = jnp.arange(length)
    rel = jnp.where(k < TQ, k, k - length)
    tiles = []
    for s in range(2):
        dist = TK * (1 - s) + rel
        v = jnp.where(dist[:, None] >= 0, _bias_lookup(rel_bias, dist), MASKED).T
        tiles.append(_toeplitz(v, TK)[:, :, :TQ])
    far = rel_bias[N_BUCKETS - 1].astype(F32)
    return far, jnp.stack(tiles, axis=1)


def _decode_bias(rel_bias, n_tok):
    def expand(b, valid):
        n_keys = b.shape[1]
        b = jnp.where(valid[:, :, None], b, MASKED)
        b = jnp.transpose(b, (0, 2, 1)).reshape(n_tok * N_HEADS, n_keys)
        b = jnp.repeat(b, N_HEADS, axis=1)
        hq = jnp.arange(n_tok * N_HEADS)[:, None] % N_HEADS
        hk = jnp.arange(n_keys * N_HEADS)[None, :] % N_HEADS
        b = jnp.where(hq == hk, b, MASKED)
        return jnp.concatenate([b, b], axis=0)

    i = jnp.arange(n_tok)[:, None]
    d_last = PAGE_SIZE + i - jnp.arange(PAGE_SIZE)[None, :]
    d_new = i - jnp.arange(n_tok)[None, :]
    far = expand(jnp.broadcast_to(rel_bias[N_BUCKETS - 1].astype(F32), (n_tok, PAGE_SIZE, N_HEADS)),
                 jnp.ones((n_tok, PAGE_SIZE), bool))
    last = expand(_bias_lookup(rel_bias, d_last), d_last >= 0)
    new = expand(_bias_lookup(rel_bias, d_new), d_new >= 0)
    return far, last, new


def kernel(x_prompt, x_sample, cache_k, cache_v, state_conv, state_h, page_table, rel_bias,
           norm_g, w_in, lam_vecs, subln_g, conv_w, conv_b, w_a, b_a, w_i, b_i, lru_lam,
           w_pa, w_pr, w_out, final_norm_g):
    batch, seq, _ = x_prompt.shape
    n_seq, n_tok, _ = x_sample.shape
    depth = w_in.shape[0]
    assert seq % ATT_TQ == 0 and ATT_TQ % ATT_TK == 0
    assert seq % POST_TILE == 0 and (batch * seq) % PROJ_TILE == 0
    assert page_table.shape[1] % PAGES_PER_STEP == 0 and cache_k.shape[2] == PAGE_SIZE
    assert ATT_TK >= MAX_DIST and PAGE_SIZE >= MAX_DIST

    cfar, bias_near = _prompt_bias(rel_bias)
    dec_far, dec_last, dec_new = _decode_bias(rel_bias, n_tok)

    yp = x_prompt.reshape(batch * seq, D_MODEL)
    ys = jnp.swapaxes(x_sample, 0, 1).reshape(n_tok * n_seq, D_MODEL)
    row = lambda v: v.reshape(1, -1)
    outs = [[] for _ in range(8)]

    for l in range(depth):
        lam_init = 0.8 - 0.6 * math.exp(-0.3 * l)
        final = l == depth - 1
        w_in_bf = w_in[l].astype(BF16)
        wai = jnp.concatenate([w_a[l], w_i[l]], axis=-1).astype(BF16)
        weights = (conv_w[l], row(conv_b[l]), wai, row(b_a[l]), row(b_i[l]), row(lru_lam[l]),
                   w_pa[l].astype(BF16), w_pr[l].astype(BF16), w_out[l].astype(BF16),
                   row(final_norm_g))
        g_in, sg = row(norm_g[l]), row(subln_g[l])

        p, kf, vf = _inproj(yp, g_in, w_in_bf, PROJ_TILE)
        o = _attn_prompt(p, cfar, bias_near, lam_vecs[l], sg, lam_init, batch, seq)
        yp, conv_t, h_t = _post_prompt(yp, o, p, weights, final, batch, seq)
        outs[0].append(kf.reshape(batch, seq, N_HEADS, V_DIM))
        outs[1].append(vf.reshape(batch, seq, N_HEADS, V_DIM))
        outs[2].append(conv_t[:, 8 - (CONV_W - 1):, :])
        outs[3].append(h_t[:, 7, :])

        ps, kfs, vfs = _inproj(ys, g_in, w_in_bf, n_tok * n_seq)
        seq_major = lambda v: jnp.swapaxes(v.reshape(n_tok, n_seq, N_HEADS, V_DIM), 0, 1)
        q_s = seq_major(ps[:, :D_MODEL]).reshape(n_seq, n_tok * N_HEADS, V_DIM)
        k_s, v_s = seq_major(kfs), seq_major(vfs)
        o_s = _attn_decode(l, page_table, q_s,
                           k_s.reshape(n_seq, n_tok * N_HEADS, V_DIM).astype(BF16),
                           v_s.reshape(n_seq, n_tok * N_HEADS, V_DIM).astype(BF16),
                           cache_k, cache_v, dec_far, dec_last, dec_new, lam_vecs[l], sg, lam_init)
        o_s = jnp.swapaxes(o_s.reshape(n_seq, n_tok, D_MODEL), 0, 1).reshape(n_tok * n_seq, D_MODEL)
        ys, conv_s, h_s = _post_sample(ys, o_s, ps, jnp.swapaxes(state_conv[l], 0, 1), state_h[l],
                                       weights, final, n_seq, n_tok)
        outs[4].append(k_s)
        outs[5].append(v_s)
        outs[6].append(jnp.swapaxes(conv_s, 0, 1))
        outs[7].append(h_s)

    y_prompt = yp.reshape(batch, seq, D_MODEL)
    y_sample = jnp.swapaxes(ys.reshape(n_tok, n_seq, D_MODEL), 0, 1)
    return (y_prompt, y_sample) + tuple(jnp.stack(o) for o in outs)
```

```python
import functools
import math

import jax
import jax.numpy as jnp
from jax import lax
from jax.experimental import pallas as pl
from jax.experimental.pallas import tpu as pltpu

F32 = jnp.float32
BF16 = jnp.bfloat16

D_MODEL = 1024
N_HEADS = 8
HEAD_DIM = 64
V_DIM = 2 * HEAD_DIM
D_RNN = 1024
N_BLOCKS = 8
BLOCK_W = D_RNN // N_BLOCKS
CONV_W = 4
RG_C = 8.0
N_BUCKETS = 32
MAX_DIST = 128
PAGE_SIZE = 128
EPS = 1e-6
ATT_SCALE = HEAD_DIM ** -0.5
N_PROJ = 8
COL_Q, COL_K, COL_V, COL_ZA, COL_XR, COL_ZR, COL_GA, COL_GR = range(N_PROJ)
MASKED = -1e30

ATT_TQ = 2048
ATT_TK = 256
ONES_ROWS = 16
SUBLANES = 8
POST_TILE = 256
PROJ_TILE = 1024
PAGES_PER_STEP = 16
DECODE_GROUPS = 1
VMEM_LIMIT = 56 * 1024 * 1024


def _params(n_axes):
    return pltpu.CompilerParams(dimension_semantics=("arbitrary",) * n_axes,
                                vmem_limit_bytes=VMEM_LIMIT)


def _sigmoid(x):
    return 1.0 / (1.0 + jnp.exp(-x))


def _silu(x):
    return x * _sigmoid(x)


def _softplus(x):
    return jnp.maximum(x, 0.0) + jnp.log1p(jnp.exp(-jnp.abs(x)))


def _diff_lambda(lv_ref, lam_init):
    a = jnp.sum(lv_ref[0:1, :] * lv_ref[1:2, :], axis=1, keepdims=True)
    b = jnp.sum(lv_ref[2:3, :] * lv_ref[3:4, :], axis=1, keepdims=True)
    return jnp.exp(a) - jnp.exp(b) + lam_init


def _split_maps(q):
    lane = lax.broadcasted_iota(jnp.int32, q.shape, 1)
    qs = q * jnp.asarray(ATT_SCALE, q.dtype)
    zero = jnp.zeros_like(qs)
    return jnp.concatenate([jnp.where(lane < HEAD_DIM, qs, zero),
                            jnp.where(lane >= HEAD_DIM, qs, zero)], axis=0)


def _subln(d, g, lam_init):
    ms = jnp.mean(d * d, axis=-1, keepdims=True)
    return d * lax.rsqrt(ms + EPS) * g * (1.0 - lam_init)


def _inproj_kernel(x_ref, g_ref, w_ref, p_ref, kf_ref, vf_ref, u_scr):
    j = pl.program_id(1)

    @pl.when(j == 0)
    def _():
        x = x_ref[...]
        ms = jnp.mean(x * x, axis=-1, keepdims=True)
        u_scr[...] = (x * lax.rsqrt(ms + EPS) * g_ref[...]).astype(BF16)

    acc = jnp.dot(u_scr[...], w_ref[...], preferred_element_type=F32)
    p_ref[...] = acc.astype(BF16)

    @pl.when(j == COL_K)
    def _():
        kf_ref[...] = acc

    @pl.when(j == COL_V)
    def _():
        vf_ref[...] = acc


def _inproj(x, g, w_bf, tile):
    n = x.shape[0]
    return pl.pallas_call(
        _inproj_kernel,
        grid=(n // tile, N_PROJ),
        in_specs=[
            pl.BlockSpec((tile, D_MODEL), lambda i, j: (i, 0)),
            pl.BlockSpec((1, D_MODEL), lambda i, j: (0, 0)),
            pl.BlockSpec((D_MODEL, D_MODEL), lambda i, j: (0, j)),
        ],
        out_specs=[
            pl.BlockSpec((tile, D_MODEL), lambda i, j: (i, j)),
            pl.BlockSpec((tile, D_MODEL), lambda i, j: (i, 0)),
            pl.BlockSpec((tile, D_MODEL), lambda i, j: (i, 0)),
        ],
        out_shape=[
            jax.ShapeDtypeStruct((n, N_PROJ * D_MODEL), BF16),
            jax.ShapeDtypeStruct((n, D_MODEL), F32),
            jax.ShapeDtypeStruct((n, D_MODEL), F32),
        ],
        scratch_shapes=[pltpu.VMEM((tile, D_MODEL), BF16)],
        compiler_params=_params(2),
        name="inproj",
    )(x, g, w_bf)


def _attn_prompt_kernel(cfar_ref, q_ref, k_ref, v_ref, bias_ref, lv_ref, sg_ref,
                        o_ref, vte_scr, m_scr, acc_scr, *, lam_init, seq):
    TQ, TK = ATT_TQ, ATT_TK
    ratio = TQ // TK

    for j in range(seq // TK):
        vt = v_ref[j * TK:(j + 1) * TK, :].astype(F32).T
        vte_scr[j, 0:V_DIM, :] = vt.astype(BF16)
        vte_scr[j, V_DIM:, :] = jnp.ones((ONES_ROWS, TK), BF16)

    lam = _diff_lambda(lv_ref, lam_init)
    c_far = cfar_ref[...]

    def q_tile(qi, carry):
        q_start = pl.multiple_of(qi * TQ, TQ)
        qq = _split_maps(q_ref[pl.ds(q_start, TQ), :])
        m_scr[...] = jnp.full(m_scr.shape, MASKED, F32)
        acc_scr[...] = jnp.zeros(acc_scr.shape, F32)

        def step(kj, bias_tile, lo):
            w = TQ - lo
            kt = k_ref[pl.ds(pl.multiple_of(kj * TK, TK), TK), :]
            qs = jnp.concatenate([qq[lo:TQ], qq[TQ + lo:]], axis=0)
            st = lax.dot_general(kt, qs, (((1,), (1,)), ((), ())), preferred_element_type=F32)
            m_old = jnp.concatenate([m_scr[0, :, lo:], m_scr[1, :, lo:]], axis=1)
            if bias_tile is None:
                m_new = jnp.maximum(m_old, jnp.max(st, axis=0, keepdims=True) + c_far)
                shift = m_new - c_far
            else:
                b = bias_tile[:, :w]
                st = st + jnp.concatenate([b, b], axis=1)
                m_new = jnp.maximum(m_old, jnp.max(st, axis=0, keepdims=True))
                shift = m_new
            alpha = jnp.exp(m_old - m_new)
            p = jnp.exp(st - shift).astype(BF16)
            pv = jnp.dot(vte_scr[kj], p, preferred_element_type=F32)
            for mp in range(2):
                cols = slice(mp * w, (mp + 1) * w)
                acc_scr[mp, :, lo:] = alpha[:, cols] * acc_scr[mp, :, lo:] + pv[:, cols]
                m_scr[mp, :, lo:] = m_new[:, cols]

        def far_body(kj, c):
            step(kj, None, 0)
            return c

        first_near = qi * ratio - 1
        lax.fori_loop(0, first_near, far_body, 0)

        @pl.when(qi >= 1)
        def _():
            step(first_near, bias_ref[0], 0)

        for s in range(ratio):
            step(qi * ratio + s, bias_ref[1], s * TK)

        o = [acc_scr[mp, 0:V_DIM, :] / acc_scr[mp, V_DIM:V_DIM + 1, :] for mp in range(2)]
        d = (o[0] - lam * o[1]).T
        o_ref[pl.ds(q_start, TQ), :] = _subln(d, sg_ref[...], lam_init).astype(o_ref.dtype)
        return carry

    lax.fori_loop(0, seq // TQ, q_tile, 0)


def _attn_prompt(p, cfar, bias_near, lam_vecs, subln_g, lam_init, batch, seq):
    TQ, TK = ATT_TQ, ATT_TK
    n_near = bias_near.shape[1]
    return pl.pallas_call(
        functools.partial(_attn_prompt_kernel, lam_init=lam_init, seq=seq),
        grid=(N_HEADS, batch),
        in_specs=[
            pl.BlockSpec((None, 1, 1), lambda h, b: (h, 0, 0)),
            pl.BlockSpec((seq, V_DIM), lambda h, b: (b, COL_Q * N_HEADS + h)),
            pl.BlockSpec((seq, V_DIM), lambda h, b: (b, COL_K * N_HEADS + h)),
            pl.BlockSpec((seq, V_DIM), lambda h, b: (b, COL_V * N_HEADS + h)),
            pl.BlockSpec((None, n_near, TK, TQ), lambda h, b: (h, 0, 0, 0)),
            pl.BlockSpec((4, HEAD_DIM), lambda h, b: (0, 0)),
            pl.BlockSpec((1, V_DIM), lambda h, b: (0, 0)),
        ],
        out_specs=pl.BlockSpec((seq, V_DIM), lambda h, b: (b, h)),
        out_shape=jax.ShapeDtypeStruct((batch * seq, N_HEADS * V_DIM), BF16),
        scratch_shapes=[pltpu.VMEM((seq // TK, V_DIM + ONES_ROWS, TK), BF16),
                        pltpu.VMEM((2, 1, TQ), F32),
                        pltpu.VMEM((2, V_DIM + ONES_ROWS, TQ), F32)],
        compiler_params=_params(2),
        name="attn_prompt",
    )(cfar.reshape(N_HEADS, 1, 1), p, p, p, bias_near, lam_vecs, subln_g)


def _attn_decode_kernel(pt_ref, q_ref, kn_ref, vn_ref, bfar_ref, blast_ref, bnew_ref, lv_ref, sg_ref,
                        *rest, lam_init):
    G = PAGES_PER_STEP
    k_refs = rest[:G]
    v_refs = rest[G:2 * G]
    o_ref = rest[2 * G]
    wq_scr, m_scr, l_scr, acc_scr = rest[2 * G + 1:]
    c = pl.program_id(1)
    n_c = pl.num_programs(1)
    rows = PAGE_SIZE * N_HEADS

    @pl.when(c == 0)
    def _():
        wq_scr[...] = _split_maps(q_ref[...])
        m_scr[...] = jnp.full(m_scr.shape, MASKED, F32)
        l_scr[...] = jnp.zeros(l_scr.shape, F32)
        acc_scr[...] = jnp.zeros(acc_scr.shape, F32)

    wq = wq_scr[...]

    def update(keys, values, biases):
        n_groups = min(DECODE_GROUPS, len(keys))
        size = len(keys) // n_groups
        parts = []
        for g in range(n_groups):
            blk = slice(g * size, (g + 1) * size)
            s = [lax.dot_general(wq, k, (((1,), (1,)), ((), ())), preferred_element_type=F32) + b
                 for k, b in zip(keys[blk], biases[blk])]
            m_grp = jnp.max(functools.reduce(jnp.maximum, s), axis=-1, keepdims=True)
            p = [jnp.exp(x - m_grp) for x in s]
            l_grp = jnp.sum(functools.reduce(jnp.add, p), axis=-1, keepdims=True)
            pv = [jnp.dot(x.astype(BF16), v, preferred_element_type=F32) for x, v in zip(p, values[blk])]
            parts.append((m_grp, l_grp, functools.reduce(jnp.add, pv)))
        m_old = m_scr[...]
        m_new = functools.reduce(jnp.maximum, [m_grp for m_grp, _, _ in parts], m_old)
        alpha = jnp.exp(m_old - m_new)
        l_sum = alpha * l_scr[...]
        acc = alpha * acc_scr[...]
        for m_grp, l_grp, a_grp in parts:
            w = jnp.exp(m_grp - m_new)
            l_sum = l_sum + w * l_grp
            acc = acc + w * a_grp
        l_scr[...] = l_sum
        acc_scr[...] = acc
        m_scr[...] = m_new

    bias_far = bfar_ref[...]
    update([k_refs[g][...].reshape(rows, V_DIM).astype(BF16) for g in range(G)],
           [v_refs[g][...].reshape(rows, V_DIM).astype(BF16) for g in range(G)],
           [bias_far] * (G - 1) + [jnp.where(c == n_c - 1, blast_ref[...], bias_far)])

    @pl.when(c == n_c - 1)
    def _():
        update([kn_ref[...]], [vn_ref[...]], [bnew_ref[...]])
        o = acc_scr[...] / l_scr[...]
        half = o.shape[0] // 2
        lam = _diff_lambda(lv_ref, lam_init)
        d = o[:half] - lam * o[half:]
        o_ref[...] = _subln(d, sg_ref[...], lam_init).astype(o_ref.dtype)


def _attn_decode(layer, page_table, q, k_new, v_new, cache_k, cache_v, bias_far, bias_last, bias_new,
                 lam_vecs, subln_g, lam_init):
    G = PAGES_PER_STEP
    n_seq, n_pages = page_table.shape
    tok_heads = q.shape[1]
    rows = PAGE_SIZE * N_HEADS

    def page_spec(g):
        return pl.BlockSpec((None, None, PAGE_SIZE, N_HEADS, V_DIM),
                            lambda s, c, pt: (layer, pt[s, c * G + g], 0, 0, 0))

    def seq_spec(r):
        return pl.BlockSpec((None, r, V_DIM), lambda s, c, pt: (s, 0, 0))

    def const_spec(shape):
        return pl.BlockSpec(shape, lambda s, c, pt: (0,) * len(shape))

    grid_spec = pltpu.PrefetchScalarGridSpec(
        num_scalar_prefetch=1,
        grid=(n_seq, n_pages // G),
        in_specs=[seq_spec(tok_heads), seq_spec(tok_heads), seq_spec(tok_heads),
                  const_spec((2 * tok_heads, rows)), const_spec((2 * tok_heads, rows)),
                  const_spec((2 * tok_heads, tok_heads)),
                  const_spec((4, HEAD_DIM)), const_spec((1, V_DIM))]
                 + [page_spec(g) for g in range(G)] + [page_spec(g) for g in range(G)],
        out_specs=seq_spec(tok_heads),
        scratch_shapes=[pltpu.VMEM((2 * tok_heads, V_DIM), BF16),
                        pltpu.VMEM((2 * tok_heads, 1), F32), pltpu.VMEM((2 * tok_heads, 1), F32),
                        pltpu.VMEM((2 * tok_heads, V_DIM), F32)],
    )
    return pl.pallas_call(
        functools.partial(_attn_decode_kernel, lam_init=lam_init),
        grid_spec=grid_spec,
        out_shape=jax.ShapeDtypeStruct((n_seq, tok_heads, V_DIM), BF16),
        compiler_params=_params(2),
        name="attn_decode",
    )(page_table, q, k_new, v_new, bias_far, bias_last, bias_new, lam_vecs, subln_g,
      *([cache_k] * G), *([cache_v] * G))


def _gates(xc, wai_ref, ba_ref, bi_ref):
    xb = xc.astype(BF16)
    r_parts, i_parts = [], []
    for n in range(N_BLOCKS):
        sl = slice(n * BLOCK_W, (n + 1) * BLOCK_W)
        g = jnp.dot(xb[:, sl], wai_ref[n], preferred_element_type=F32)
        r_parts.append(g[:, :BLOCK_W])
        i_parts.append(g[:, BLOCK_W:])
    r = _sigmoid(jnp.concatenate(r_parts, axis=1) + ba_ref[...])
    i = _sigmoid(jnp.concatenate(i_parts, axis=1) + bi_ref[...])
    return r, i


def _lru_coeffs(xc, r, i, lam_ref):
    log_a = -RG_C * r * _softplus(-lam_ref[...])
    a = jnp.exp(log_a)
    gap = 1.0 - a * a
    u = jnp.where(gap > 0.0, gap * lax.rsqrt(gap), 0.0) * (i * xc)
    return a, u


def _merge_out(x, o, za, h, zr, ga, gr, wpa_ref, wpr_ref, wout_ref):
    branch_a = jnp.dot((o * _silu(za)).astype(BF16), wpa_ref[...], preferred_element_type=F32)
    branch_r = jnp.dot((h * _silu(zr)).astype(BF16), wpr_ref[...], preferred_element_type=F32)
    merged = _sigmoid(ga) * branch_a + _sigmoid(gr) * branch_r
    return x + jnp.dot(merged.astype(BF16), wout_ref[...], preferred_element_type=F32)


def _final_norm(y, g_ref):
    ms = jnp.mean(y * y, axis=-1, keepdims=True)
    return y * lax.rsqrt(ms + EPS) * g_ref[...]


def _post_prompt_kernel(x_ref, o_ref, za_ref, xr_ref, zr_ref, ga_ref, gr_ref,
                        cw_ref, cb_ref, wai_ref, ba_ref, bi_ref, lam_ref,
                        wpa_ref, wpr_ref, wout_ref, fg_ref,
                        y_ref, conv_ref, h_ref, xprev_scr, hprev_scr, *, final):
    T = POST_TILE
    t = pl.program_id(1)

    @pl.when(t == 0)
    def _():
        xprev_scr[...] = jnp.zeros(xprev_scr.shape, F32)
        hprev_scr[...] = jnp.zeros(hprev_scr.shape, F32)

    xr = xr_ref[...].astype(F32)
    row8 = lax.broadcasted_iota(jnp.int32, (8, D_RNN), 0)
    xprev = xprev_scr[...]

    t_out = lax.broadcasted_iota(jnp.int32, (T, T), 0)
    t_in = lax.broadcasted_iota(jnp.int32, (T, T), 1)
    xb = xr_ref[...]

    def delayed(k):
        shift = (t_out - t_in == k).astype(BF16)
        r = jnp.dot(shift, xb, preferred_element_type=F32)
        top = jnp.where(row8 < k, pltpu.roll(xprev, k, 0), r[0:8])
        return jnp.concatenate([top, r[8:]], axis=0)

    xc = cb_ref[...] + cw_ref[3:4, :] * xr
    for k in range(1, CONV_W):
        xc = xc + cw_ref[CONV_W - 1 - k:CONV_W - k, :] * delayed(k)

    r, i = _gates(xc, wai_ref, ba_ref, bi_ref)
    a, u = _lru_coeffs(xc, r, i, lam_ref)

    grouped = (T // SUBLANES, SUBLANES, D_RNN)
    a = a.reshape(grouped)
    u = u.reshape(grouped)
    sub = lax.broadcasted_iota(jnp.int32, grouped, 1)
    d = 1
    while d < SUBLANES:
        keep = sub >= d
        u = jnp.where(keep, a * pltpu.roll(u, d, 1) + u, u)
        a = jnp.where(keep, a * pltpu.roll(a, d, 1), a)
        d *= 2
    a = a.reshape(T, D_RNN)
    u = u.reshape(T, D_RNN)
    carry = hprev_scr[SUBLANES - 1:SUBLANES, :]
    groups = []
    for g in range(T // SUBLANES):
        rows = slice(g * SUBLANES, (g + 1) * SUBLANES)
        hg = u[rows] + a[rows] * carry
        groups.append(hg)
        carry = hg[SUBLANES - 1:SUBLANES, :]
    h = jnp.concatenate(groups, axis=0)

    xprev_scr[...] = xr[T - 8:, :]
    hprev_scr[...] = h[T - 8:, :]
    conv_ref[...] = xr[T - 8:, :]
    h_ref[...] = h[T - 8:, :]

    y = _merge_out(x_ref[...], o_ref[...].astype(F32), za_ref[...].astype(F32), h,
                   zr_ref[...].astype(F32), ga_ref[...].astype(F32), gr_ref[...].astype(F32),
                   wpa_ref, wpr_ref, wout_ref)
    if final:
        y = _final_norm(y, fg_ref)
    y_ref[...] = y


def _post_prompt(x, o, p, weights, final, batch, seq):
    T = POST_TILE
    nt = seq // T

    def col_spec(col):
        return pl.BlockSpec((T, D_MODEL), lambda b, t: (b * nt + t, col))

    def const_spec(shape):
        return pl.BlockSpec(shape, lambda b, t: (0,) * len(shape))

    tail_spec = pl.BlockSpec((None, 8, D_RNN), lambda b, t: (b, 0, 0))
    (cw, cb, wai, ba, bi, lam, wpa, wpr, wout, fg) = weights
    return pl.pallas_call(
        functools.partial(_post_prompt_kernel, final=final),
        grid=(batch, nt),
        in_specs=[col_spec(0), col_spec(0), col_spec(COL_ZA), col_spec(COL_XR), col_spec(COL_ZR),
                  col_spec(COL_GA), col_spec(COL_GR),
                  const_spec(cw.shape), const_spec(cb.shape), const_spec(wai.shape),
                  const_spec(ba.shape), const_spec(bi.shape), const_spec(lam.shape),
                  const_spec(wpa.shape), const_spec(wpr.shape), const_spec(wout.shape),
                  const_spec(fg.shape)],
        out_specs=[col_spec(0), tail_spec, tail_spec],
        out_shape=[jax.ShapeDtypeStruct((batch * seq, D_MODEL), F32),
                   jax.ShapeDtypeStruct((batch, 8, D_RNN), F32),
                   jax.ShapeDtypeStruct((batch, 8, D_RNN), F32)],
        scratch_shapes=[pltpu.VMEM((8, D_RNN), F32), pltpu.VMEM((8, D_RNN), F32)],
        compiler_params=_params(2),
        name="post_prompt",
    )(x, o, p, p, p, p, p, cw, cb, wai, ba, bi, lam, wpa, wpr, wout, fg)


def _post_sample_kernel(x_ref, o_ref, za_ref, xr_ref, zr_ref, ga_ref, gr_ref, buf_ref, h0_ref,
                        cw_ref, cb_ref, wai_ref, ba_ref, bi_ref, lam_ref,
                        wpa_ref, wpr_ref, wout_ref, fg_ref,
                        y_ref, conv_ref, h_ref, *, final, n_seq, n_tok):
    xr = xr_ref[...].astype(F32)
    hist = [buf_ref[k] for k in range(CONV_W - 1)]
    hist += [xr[t * n_seq:(t + 1) * n_seq, :] for t in range(n_tok)]
    xc_parts = []
    for t in range(n_tok):
        acc = cb_ref[...]
        for j in range(CONV_W):
            acc = acc + cw_ref[j:j + 1, :] * hist[t + j]
        xc_parts.append(acc)
    xc = jnp.concatenate(xc_parts, axis=0)
    for k in range(CONV_W - 1):
        conv_ref[k] = hist[n_tok + k]

    r, i = _gates(xc, wai_ref, ba_ref, bi_ref)
    a, u = _lru_coeffs(xc, r, i, lam_ref)
    h = h0_ref[...]
    h_parts = []
    for t in range(n_tok):
        sl = slice(t * n_seq, (t + 1) * n_seq)
        h = a[sl, :] * h + u[sl, :]
        h_parts.append(h)
    h_ref[...] = h
    h_all = jnp.concatenate(h_parts, axis=0)

    y = _merge_out(x_ref[...], o_ref[...].astype(F32), za_ref[...].astype(F32), h_all,
                   zr_ref[...].astype(F32), ga_ref[...].astype(F32), gr_ref[...].astype(F32),
                   wpa_ref, wpr_ref, wout_ref)
    if final:
        y = _final_norm(y, fg_ref)
    y_ref[...] = y


def _post_sample(x, o, p, buf, h0, weights, final, n_seq, n_tok):
    n = n_seq * n_tok

    def col_spec(col):
        return pl.BlockSpec((n, D_MODEL), lambda i: (0, col))

    def const_spec(shape):
        return pl.BlockSpec(shape, lambda i: (0,) * len(shape))

    (cw, cb, wai, ba, bi, lam, wpa, wpr, wout, fg) = weights
    return pl.pallas_call(
        functools.partial(_post_sample_kernel, final=final, n_seq=n_seq, n_tok=n_tok),
        grid=(1,),
        in_specs=[col_spec(0), col_spec(0), col_spec(COL_ZA), col_spec(COL_XR), col_spec(COL_ZR),
                  col_spec(COL_GA), col_spec(COL_GR), const_spec(buf.shape), const_spec(h0.shape),
                  const_spec(cw.shape), const_spec(cb.shape), const_spec(wai.shape),
                  const_spec(ba.shape), const_spec(bi.shape), const_spec(lam.shape),
                  const_spec(wpa.shape), const_spec(wpr.shape), const_spec(wout.shape),
                  const_spec(fg.shape)],
        out_specs=[col_spec(0), const_spec(buf.shape), const_spec(h0.shape)],
        out_shape=[jax.ShapeDtypeStruct((n, D_MODEL), F32),
                   jax.ShapeDtypeStruct(buf.shape, F32),
                   jax.ShapeDtypeStruct(h0.shape, F32)],
        compiler_params=_params(1),
        name="post_sample",
    )(x, o, p, p, p, p, p, buf, h0, cw, cb, wai, ba, bi, lam, wpa, wpr, wout, fg)


def _t5_bucket(dist):
    n = jnp.maximum(dist, 0)
    max_exact = N_BUCKETS // 2
    nf = jnp.maximum(n, 1).astype(F32)
    large = max_exact + (jnp.log(nf / max_exact) / math.log(MAX_DIST / max_exact)
                         * (N_BUCKETS - max_exact)).astype(jnp.int32)
    large = jnp.minimum(large, N_BUCKETS - 1)
    return jnp.where(n < max_exact, n, large)


def _bias_lookup(rel_bias, dist):
    return rel_bias[_t5_bucket(dist)].astype(F32)


def _toeplitz(v, rows):
    n_heads, length = v.shape
    return jnp.tile(v, (1, rows))[:, :rows * (length - 1)].reshape(n_heads, rows, length - 1)


def _prompt_bias(rel_bias):
    TQ, TK = ATT_TQ, ATT_TK
    length = TQ + TK
    k = jnp.arange(length)
    rel = jnp.where(k < TQ, k, k - length)
    tiles = []
    for s in range(2):
        dist = TK * (1 - s) + rel
        v = jnp.where(dist[:, None] >= 0, _bias_lookup(rel_bias, dist), MASKED).T
        tiles.append(_toeplitz(v, TK)[:, :, :TQ])
    far = rel_bias[N_BUCKETS - 1].astype(F32)
    return far, jnp.stack(tiles, axis=1)


def _decode_bias(rel_bias, n_tok):
    def expand(b, valid):
        n_keys = b.shape[1]
        b = jnp.where(valid[:, :, None], b, MASKED)
        b = jnp.transpose(b, (0, 2, 1)).reshape(n_tok * N_HEADS, n_keys)
        b = jnp.repeat(b, N_HEADS, axis=1)
        hq = jnp.arange(n_tok * N_HEADS)[:, None] % N_HEADS
        hk = jnp.arange(n_keys * N_HEADS)[None, :] % N_HEADS
        b = jnp.where(hq == hk, b, MASKED)
        return jnp.concatenate([b, b], axis=0)

    i = jnp.arange(n_tok)[:, None]
    d_last = PAGE_SIZE + i - jnp.arange(PAGE_SIZE)[None, :]
    d_new = i - jnp.arange(n_tok)[None, :]
    far = expand(jnp.broadcast_to(rel_bias[N_BUCKETS - 1].astype(F32), (n_tok, PAGE_SIZE, N_HEADS)),
                 jnp.ones((n_tok, PAGE_SIZE), bool))
    last = expand(_bias_lookup(rel_bias, d_last), d_last >= 0)
    new = expand(_bias_lookup(rel_bias, d_new), d_new >= 0)
    return far, last, new


def kernel(x_prompt, x_sample, cache_k, cache_v, state_conv, state_h, page_table, rel_bias,
           norm_g, w_in, lam_vecs, subln_g, conv_w, conv_b, w_a, b_a, w_i, b_i, lru_lam,
           w_pa, w_pr, w_out, final_norm_g):
    batch, seq, _ = x_prompt.shape
    n_seq, n_tok, _ = x_sample.shape
    depth = w_in.shape[0]
    assert seq % ATT_TQ == 0 and ATT_TQ % ATT_TK == 0
    assert seq % POST_TILE == 0 and (batch * seq) % PROJ_TILE == 0
    assert page_table.shape[1] % PAGES_PER_STEP == 0 and cache_k.shape[2] == PAGE_SIZE
    assert ATT_TK >= MAX_DIST and PAGE_SIZE >= MAX_DIST

    cfar, bias_near = _prompt_bias(rel_bias)
    dec_far, dec_last, dec_new = _decode_bias(rel_bias, n_tok)

    yp = x_prompt.reshape(batch * seq, D_MODEL)
    ys = jnp.swapaxes(x_sample, 0, 1).reshape(n_tok * n_seq, D_MODEL)
    row = lambda v: v.reshape(1, -1)
    outs = [[] for _ in range(8)]

    for l in range(depth):
        lam_init = 0.8 - 0.6 * math.exp(-0.3 * l)
        final = l == depth - 1
        w_in_bf = w_in[l].astype(BF16)
        wai = jnp.concatenate([w_a[l], w_i[l]], axis=-1).astype(BF16)
        weights = (conv_w[l], row(conv_b[l]), wai, row(b_a[l]), row(b_i[l]), row(lru_lam[l]),
                   w_pa[l].astype(BF16), w_pr[l].astype(BF16), w_out[l].astype(BF16),
                   row(final_norm_g))
        g_in, sg = row(norm_g[l]), row(subln_g[l])

        p, kf, vf = _inproj(yp, g_in, w_in_bf, PROJ_TILE)
        o = _attn_prompt(p, cfar, bias_near, lam_vecs[l], sg, lam_init, batch, seq)
        yp, conv_t, h_t = _post_prompt(yp, o, p, weights, final, batch, seq)
        outs[0].append(kf.reshape(batch, seq, N_HEADS, V_DIM))
        outs[1].append(vf.reshape(batch, seq, N_HEADS, V_DIM))
        outs[2].append(conv_t[:, 8 - (CONV_W - 1):, :])
        outs[3].append(h_t[:, 7, :])

        ps, kfs, vfs = _inproj(ys, g_in, w_in_bf, n_tok * n_seq)
        seq_major = lambda v: jnp.swapaxes(v.reshape(n_tok, n_seq, N_HEADS, V_DIM), 0, 1)
        q_s = seq_major(ps[:, :D_MODEL]).reshape(n_seq, n_tok * N_HEADS, V_DIM)
        k_s, v_s = seq_major(kfs), seq_major(vfs)
        o_s = _attn_decode(l, page_table, q_s,
                           k_s.reshape(n_seq, n_tok * N_HEADS, V_DIM).astype(BF16),
                           v_s.reshape(n_seq, n_tok * N_HEADS, V_DIM).astype(BF16),
                           cache_k, cache_v, dec_far, dec_last, dec_new, lam_vecs[l], sg, lam_init)
        o_s = jnp.swapaxes(o_s.reshape(n_seq, n_tok, D_MODEL), 0, 1).reshape(n_tok * n_seq, D_MODEL)
        ys, conv_s, h_s = _post_sample(ys, o_s, ps, jnp.swapaxes(state_conv[l], 0, 1), state_h[l],
                                       weights, final, n_seq, n_tok)
        outs[4].append(k_s)
        outs[5].append(v_s)
        outs[6].append(jnp.swapaxes(conv_s, 0, 1))
        outs[7].append(h_s)

    y_prompt = yp.reshape(batch, seq, D_MODEL)
    y_sample = jnp.swapaxes(ys.reshape(n_tok, n_seq, D_MODEL), 0, 1)
    return (y_prompt, y_sample) + tuple(jnp.stack(o) for o in outs)
```

```python
import functools
import math

import jax
import jax.numpy as jnp
from jax import lax
from jax.experimental import pallas as pl
from jax.experimental.pallas import tpu as pltpu

F32 = jnp.float32
BF16 = jnp.bfloat16

D_MODEL = 1024
N_HEADS = 8
HEAD_DIM = 64
V_DIM = 2 * HEAD_DIM
D_RNN = 1024
N_BLOCKS = 8
BLOCK_W = D_RNN // N_BLOCKS
CONV_W = 4
RG_C = 8.0
N_BUCKETS = 32
MAX_DIST = 128
PAGE_SIZE = 128
EPS = 1e-6
ATT_SCALE = HEAD_DIM ** -0.5
N_PROJ = 8
COL_Q, COL_K, COL_V, COL_ZA, COL_XR, COL_ZR, COL_GA, COL_GR = range(N_PROJ)
MASKED = -1e30

ATT_TQ = 2048
ATT_TK = 256
ONES_ROWS = 16
SUBLANES = 8
POST_TILE = 256
PROJ_TILE = 1024
PAGES_PER_STEP = 16
DECODE_GROUPS = 1
VMEM_LIMIT = 56 * 1024 * 1024


def _params(n_axes):
    return pltpu.CompilerParams(dimension_semantics=("arbitrary",) * n_axes,
                                vmem_limit_bytes=VMEM_LIMIT)


def _sigmoid(x):
    return 1.0 / (1.0 + jnp.exp(-x))


def _silu(x):
    return x * _sigmoid(x)


def _softplus(x):
    return jnp.maximum(x, 0.0) + jnp.log1p(jnp.exp(-jnp.abs(x)))


def _diff_lambda(lv_ref, lam_init):
    a = jnp.sum(lv_ref[0:1, :] * lv_ref[1:2, :], axis=1, keepdims=True)
    b = jnp.sum(lv_ref[2:3, :] * lv_ref[3:4, :], axis=1, keepdims=True)
    return jnp.exp(a) - jnp.exp(b) + lam_init


def _split_maps(q):
    lane = lax.broadcasted_iota(jnp.int32, q.shape, 1)
    qs = q * jnp.asarray(ATT_SCALE, q.dtype)
    zero = jnp.zeros_like(qs)
    return jnp.concatenate([jnp.where(lane < HEAD_DIM, qs, zero),
                            jnp.where(lane >= HEAD_DIM, qs, zero)], axis=0)


def _subln(d, g, lam_init):
    ms = jnp.mean(d * d, axis=-1, keepdims=True)
    return d * lax.rsqrt(ms + EPS) * g * (1.0 - lam_init)


def _inproj_kernel(x_ref, g_ref, w_ref, p_ref, kf_ref, vf_ref, u_scr):
    j = pl.program_id(1)

    @pl.when(j == 0)
    def _():
        x = x_ref[...]
        ms = jnp.mean(x * x, axis=-1, keepdims=True)
        u_scr[...] = (x * lax.rsqrt(ms + EPS) * g_ref[...]).astype(BF16)

    acc = jnp.dot(u_scr[...], w_ref[...], preferred_element_type=F32)
    p_ref[...] = acc.astype(BF16)

    @pl.when(j == COL_K)
    def _():
        kf_ref[...] = acc

    @pl.when(j == COL_V)
    def _():
        vf_ref[...] = acc


def _inproj(x, g, w_bf, tile):
    n = x.shape[0]
    return pl.pallas_call(
        _inproj_kernel,
        grid=(n // tile, N_PROJ),
        in_specs=[
            pl.BlockSpec((tile, D_MODEL), lambda i, j: (i, 0)),
            pl.BlockSpec((1, D_MODEL), lambda i, j: (0, 0)),
            pl.BlockSpec((D_MODEL, D_MODEL), lambda i, j: (0, j)),
        ],
        out_specs=[
            pl.BlockSpec((tile, D_MODEL), lambda i, j: (i, j)),
            pl.BlockSpec((tile, D_MODEL), lambda i, j: (i, 0)),
            pl.BlockSpec((tile, D_MODEL), lambda i, j: (i, 0)),
        ],
        out_shape=[
            jax.ShapeDtypeStruct((n, N_PROJ * D_MODEL), BF16),
            jax.ShapeDtypeStruct((n, D_MODEL), F32),
            jax.ShapeDtypeStruct((n, D_MODEL), F32),
        ],
        scratch_shapes=[pltpu.VMEM((tile, D_MODEL), BF16)],
        compiler_params=_params(2),
        name="inproj",
    )(x, g, w_bf)


def _attn_prompt_kernel(cfar_ref, q_ref, k_ref, v_ref, bvec_ref, lv_ref, sg_ref,
                        o_ref, bias_ref, vte_scr, m_scr, acc_scr, *, lam_init, seq):
    TQ, TK = ATT_TQ, ATT_TK
    ratio = TQ // TK

    @pl.when(pl.program_id(1) == 0)
    def _():
        for s in range(bias_ref.shape[0]):
            vec = jnp.broadcast_to(bvec_ref[s], (TK, bvec_ref.shape[-1]))
            bias_ref[s] = pltpu.roll(vec, 0, 1, stride=1, stride_axis=0)[:, :TQ]

    for j in range(seq // TK):
        vt = v_ref[j * TK:(j + 1) * TK, :].astype(F32).T
        vte_scr[j, 0:V_DIM, :] = vt.astype(BF16)
        vte_scr[j, V_DIM:, :] = jnp.ones((ONES_ROWS, TK), BF16)

    lam = _diff_lambda(lv_ref, lam_init)
    c_far = cfar_ref[...]

    def q_tile(qi, carry):
        q_start = pl.multiple_of(qi * TQ, TQ)
        qq = _split_maps(q_ref[pl.ds(q_start, TQ), :])
        m_scr[...] = jnp.full(m_scr.shape, MASKED, F32)
        acc_scr[...] = jnp.zeros(acc_scr.shape, F32)

        def step(kj, bias_tile, lo):
            w = TQ - lo
            kt = k_ref[pl.ds(pl.multiple_of(kj * TK, TK), TK), :]
            qs = jnp.concatenate([qq[lo:TQ], qq[TQ + lo:]], axis=0)
            st = lax.dot_general(kt, qs, (((1,), (1,)), ((), ())), preferred_element_type=F32)
            m_old = jnp.concatenate([m_scr[0, :, lo:], m_scr[1, :, lo:]], axis=1)
            if bias_tile is None:
                m_new = jnp.maximum(m_old, jnp.max(st, axis=0, keepdims=True) + c_far)
                shift = m_new - c_far
            else:
                b = bias_tile[:, :w]
                st = st + jnp.concatenate([b, b], axis=1)
                m_new = jnp.maximum(m_old, jnp.max(st, axis=0, keepdims=True))
                shift = m_new
            alpha = jnp.exp(m_old - m_new)
            p = jnp.exp(st - shift).astype(BF16)
            pv = jnp.dot(vte_scr[kj], p, preferred_element_type=F32)
            for mp in range(2):
                cols = slice(mp * w, (mp + 1) * w)
                acc_scr[mp, :, lo:] = alpha[:, cols] * acc_scr[mp, :, lo:] + pv[:, cols]
                m_scr[mp, :, lo:] = m_new[:, cols]

        def far_body(kj, c):
            step(kj, None, 0)
            return c

        first_near = qi * ratio - 1
        lax.fori_loop(0, first_near, far_body, 0)

        @pl.when(qi >= 1)
        def _():
            step(first_near, bias_ref[0], 0)

        for s in range(ratio):
            step(qi * ratio + s, bias_ref[1], s * TK)

        o = [acc_scr[mp, 0:V_DIM, :] / acc_scr[mp, V_DIM:V_DIM + 1, :] for mp in range(2)]
        d = (o[0] - lam * o[1]).T
        o_ref[pl.ds(q_start, TQ), :] = _subln(d, sg_ref[...], lam_init).astype(o_ref.dtype)
        return carry

    lax.fori_loop(0, seq // TQ, q_tile, 0)


def _attn_prompt(p, cfar, bias_vecs, lam_vecs, subln_g, lam_init, batch, seq):
    TQ, TK = ATT_TQ, ATT_TK
    n_near, _, vec_len = bias_vecs.shape[1:]
    return pl.pallas_call(
        functools.partial(_attn_prompt_kernel, lam_init=lam_init, seq=seq),
        grid=(N_HEADS, batch),
        in_specs=[
            pl.BlockSpec((None, 1, 1), lambda h, b: (h, 0, 0)),
            pl.BlockSpec((seq, V_DIM), lambda h, b: (b, COL_Q * N_HEADS + h)),
            pl.BlockSpec((seq, V_DIM), lambda h, b: (b, COL_K * N_HEADS + h)),
            pl.BlockSpec((seq, V_DIM), lambda h, b: (b, COL_V * N_HEADS + h)),
            pl.BlockSpec((None, n_near, 1, vec_len), lambda h, b: (h, 0, 0, 0)),
            pl.BlockSpec((4, HEAD_DIM), lambda h, b: (0, 0)),
            pl.BlockSpec((1, V_DIM), lambda h, b: (0, 0)),
        ],
        out_specs=pl.BlockSpec((seq, V_DIM), lambda h, b: (b, h)),
        out_shape=jax.ShapeDtypeStruct((batch * seq, N_HEADS * V_DIM), BF16),
        scratch_shapes=[pltpu.VMEM((n_near, TK, TQ), F32),
                        pltpu.VMEM((seq // TK, V_DIM + ONES_ROWS, TK), BF16),
                        pltpu.VMEM((2, 1, TQ), F32),
                        pltpu.VMEM((2, V_DIM + ONES_ROWS, TQ), F32)],
        compiler_params=_params(2),
        name="attn_prompt",
    )(cfar.reshape(N_HEADS, 1, 1), p, p, p, bias_vecs, lam_vecs, subln_g)


def _attn_decode_kernel(pt_ref, q_ref, kn_ref, vn_ref, bfar_ref, blast_ref, bnew_ref, lv_ref, sg_ref,
                        *rest, lam_init):
    G = PAGES_PER_STEP
    k_refs = rest[:G]
    v_refs = rest[G:2 * G]
    o_ref = rest[2 * G]
    wq_scr, m_scr, l_scr, acc_scr = rest[2 * G + 1:]
    c = pl.program_id(1)
    n_c = pl.num_programs(1)
    rows = PAGE_SIZE * N_HEADS

    @pl.when(c == 0)
    def _():
        wq_scr[...] = _split_maps(q_ref[...])
        m_scr[...] = jnp.full(m_scr.shape, MASKED, F32)
        l_scr[...] = jnp.zeros(l_scr.shape, F32)
        acc_scr[...] = jnp.zeros(acc_scr.shape, F32)

    wq = wq_scr[...]

    def update(keys, values, biases):
        n_groups = min(DECODE_GROUPS, len(keys))
        size = len(keys) // n_groups
        parts = []
        for g in range(n_groups):
            blk = slice(g * size, (g + 1) * size)
            s = [lax.dot_general(wq, k, (((1,), (1,)), ((), ())), preferred_element_type=F32) + b
                 for k, b in zip(keys[blk], biases[blk])]
            m_grp = jnp.max(functools.reduce(jnp.maximum, s), axis=-1, keepdims=True)
            p = [jnp.exp(x - m_grp) for x in s]
            l_grp = jnp.sum(functools.reduce(jnp.add, p), axis=-1, keepdims=True)
            pv = [jnp.dot(x.astype(BF16), v, preferred_element_type=F32) for x, v in zip(p, values[blk])]
            parts.append((m_grp, l_grp, functools.reduce(jnp.add, pv)))
        m_old = m_scr[...]
        m_new = functools.reduce(jnp.maximum, [m_grp for m_grp, _, _ in parts], m_old)
        alpha = jnp.exp(m_old - m_new)
        l_sum = alpha * l_scr[...]
        acc = alpha * acc_scr[...]
        for m_grp, l_grp, a_grp in parts:
            w = jnp.exp(m_grp - m_new)
            l_sum = l_sum + w * l_grp
            acc = acc + w * a_grp
        l_scr[...] = l_sum
        acc_scr[...] = acc
        m_scr[...] = m_new

    bias_far = bfar_ref[...]
    update([k_refs[g][...].reshape(rows, V_DIM).astype(BF16) for g in range(G)],
           [v_refs[g][...].reshape(rows, V_DIM).astype(BF16) for g in range(G)],
           [bias_far] * (G - 1) + [jnp.where(c == n_c - 1, blast_ref[...], bias_far)])

    @pl.when(c == n_c - 1)
    def _():
        update([kn_ref[...]], [vn_ref[...]], [bnew_ref[...]])
        o = acc_scr[...] / l_scr[...]
        half = o.shape[0] // 2
        lam = _diff_lambda(lv_ref, lam_init)
        d = o[:half] - lam * o[half:]
        o_ref[...] = _subln(d, sg_ref[...], lam_init).astype(o_ref.dtype)


def _attn_decode(layer, page_table, q, k_new, v_new, cache_k, cache_v, bias_far, bias_last, bias_new,
                 lam_vecs, subln_g, lam_init):
    G = PAGES_PER_STEP
    n_seq, n_pages = page_table.shape
    tok_heads = q.shape[1]
    rows = PAGE_SIZE * N_HEADS

    def page_spec(g):
        return pl.BlockSpec((None, None, PAGE_SIZE, N_HEADS, V_DIM),
                            lambda s, c, pt: (layer, pt[s, c * G + g], 0, 0, 0))

    def seq_spec(r):
        return pl.BlockSpec((None, r, V_DIM), lambda s, c, pt: (s, 0, 0))

    def const_spec(shape):
        return pl.BlockSpec(shape, lambda s, c, pt: (0,) * len(shape))

    grid_spec = pltpu.PrefetchScalarGridSpec(
        num_scalar_prefetch=1,
        grid=(n_seq, n_pages // G),
        in_specs=[seq_spec(tok_heads), seq_spec(tok_heads), seq_spec(tok_heads),
                  const_spec((2 * tok_heads, rows)), const_spec((2 * tok_heads, rows)),
                  const_spec((2 * tok_heads, tok_heads)),
                  const_spec((4, HEAD_DIM)), const_spec((1, V_DIM))]
                 + [page_spec(g) for g in range(G)] + [page_spec(g) for g in range(G)],
        out_specs=seq_spec(tok_heads),
        scratch_shapes=[pltpu.VMEM((2 * tok_heads, V_DIM), BF16),
                        pltpu.VMEM((2 * tok_heads, 1), F32), pltpu.VMEM((2 * tok_heads, 1), F32),
                        pltpu.VMEM((2 * tok_heads, V_DIM), F32)],
    )
    return pl.pallas_call(
        functools.partial(_attn_decode_kernel, lam_init=lam_init),
        grid_spec=grid_spec,
        out_shape=jax.ShapeDtypeStruct((n_seq, tok_heads, V_DIM), BF16),
        compiler_params=_params(2),
        name="attn_decode",
    )(page_table, q, k_new, v_new, bias_far, bias_last, bias_new, lam_vecs, subln_g,
      *([cache_k] * G), *([cache_v] * G))


def _gates(xc, wai_ref, ba_ref, bi_ref):
    xb = xc.astype(BF16)
    r_parts, i_parts = [], []
    for n in range(N_BLOCKS):
        sl = slice(n * BLOCK_W, (n + 1) * BLOCK_W)
        g = jnp.dot(xb[:, sl], wai_ref[n], preferred_element_type=F32)
        r_parts.append(g[:, :BLOCK_W])
        i_parts.append(g[:, BLOCK_W:])
    r = _sigmoid(jnp.concatenate(r_parts, axis=1) + ba_ref[...])
    i = _sigmoid(jnp.concatenate(i_parts, axis=1) + bi_ref[...])
    return r, i


def _lru_coeffs(xc, r, i, lam_ref):
    log_a = -RG_C * r * _softplus(-lam_ref[...])
    a = jnp.exp(log_a)
    gap = 1.0 - a * a
    u = jnp.where(gap > 0.0, gap * lax.rsqrt(gap), 0.0) * (i * xc)
    return a, u


def _merge_out(x, o, za, h, zr, ga, gr, wpa_ref, wpr_ref, wout_ref):
    branch_a = jnp.dot((o * _silu(za)).astype(BF16), wpa_ref[...], preferred_element_type=F32)
    branch_r = jnp.dot((h * _silu(zr)).astype(BF16), wpr_ref[...], preferred_element_type=F32)
    merged = _sigmoid(ga) * branch_a + _sigmoid(gr) * branch_r
    return x + jnp.dot(merged.astype(BF16), wout_ref[...], preferred_element_type=F32)


def _final_norm(y, g_ref):
    ms = jnp.mean(y * y, axis=-1, keepdims=True)
    return y * lax.rsqrt(ms + EPS) * g_ref[...]


def _post_prompt_kernel(x_ref, o_ref, za_ref, xr_ref, zr_ref, ga_ref, gr_ref,
                        cw_ref, cb_ref, wai_ref, ba_ref, bi_ref, lam_ref,
                        wpa_ref, wpr_ref, wout_ref, fg_ref,
                        y_ref, conv_ref, h_ref, xprev_scr, hprev_scr, *, final):
    T = POST_TILE
    t = pl.program_id(1)

    @pl.when(t == 0)
    def _():
        xprev_scr[...] = jnp.zeros(xprev_scr.shape, F32)
        hprev_scr[...] = jnp.zeros(hprev_scr.shape, F32)

    xr = xr_ref[...].astype(F32)
    row8 = lax.broadcasted_iota(jnp.int32, (8, D_RNN), 0)
    xprev = xprev_scr[...]

    t_out = lax.broadcasted_iota(jnp.int32, (T, T), 0)
    t_in = lax.broadcasted_iota(jnp.int32, (T, T), 1)
    xb = xr_ref[...]

    def delayed(k):
        shift = (t_out - t_in == k).astype(BF16)
        r = jnp.dot(shift, xb, preferred_element_type=F32)
        top = jnp.where(row8 < k, pltpu.roll(xprev, k, 0), r[0:8])
        return jnp.concatenate([top, r[8:]], axis=0)

    xc = cb_ref[...] + cw_ref[3:4, :] * xr
    for k in range(1, CONV_W):
        xc = xc + cw_ref[CONV_W - 1 - k:CONV_W - k, :] * delayed(k)

    r, i = _gates(xc, wai_ref, ba_ref, bi_ref)
    a, u = _lru_coeffs(xc, r, i, lam_ref)

    grouped = (T // SUBLANES, SUBLANES, D_RNN)
    a = a.reshape(grouped)
    u = u.reshape(grouped)
    sub = lax.broadcasted_iota(jnp.int32, grouped, 1)
    d = 1
    while d < SUBLANES:
        keep = sub >= d
        u = jnp.where(keep, a * pltpu.roll(u, d, 1) + u, u)
        a = jnp.where(keep, a * pltpu.roll(a, d, 1), a)
        d *= 2
    a = a.reshape(T, D_RNN)
    u = u.reshape(T, D_RNN)
    carry = hprev_scr[SUBLANES - 1:SUBLANES, :]
    groups = []
    for g in range(T // SUBLANES):
        rows = slice(g * SUBLANES, (g + 1) * SUBLANES)
        hg = u[rows] + a[rows] * carry
        groups.append(hg)
        carry = hg[SUBLANES - 1:SUBLANES, :]
    h = jnp.concatenate(groups, axis=0)

    xprev_scr[...] = xr[T - 8:, :]
    hprev_scr[...] = h[T - 8:, :]
    conv_ref[...] = xr[T - 8:, :]
    h_ref[...] = h[T - 8:, :]

    y = _merge_out(x_ref[...], o_ref[...].astype(F32), za_ref[...].astype(F32), h,
                   zr_ref[...].astype(F32), ga_ref[...].astype(F32), gr_ref[...].astype(F32),
                   wpa_ref, wpr_ref, wout_ref)
    if final:
        y = _final_norm(y, fg_ref)
    y_ref[...] = y


def _post_prompt(x, o, p, weights, final, batch, seq):
    T = POST_TILE
    nt = seq // T

    def col_spec(col):
        return pl.BlockSpec((T, D_MODEL), lambda b, t: (b * nt + t, col))

    def const_spec(shape):
        return pl.BlockSpec(shape, lambda b, t: (0,) * len(shape))

    tail_spec = pl.BlockSpec((None, 8, D_RNN), lambda b, t: (b, 0, 0))
    (cw, cb, wai, ba, bi, lam, wpa, wpr, wout, fg) = weights
    return pl.pallas_call(
        functools.partial(_post_prompt_kernel, final=final),
        grid=(batch, nt),
        in_specs=[col_spec(0), col_spec(0), col_spec(COL_ZA), col_spec(COL_XR), col_spec(COL_ZR),
                  col_spec(COL_GA), col_spec(COL_GR),
                  const_spec(cw.shape), const_spec(cb.shape), const_spec(wai.shape),
                  const_spec(ba.shape), const_spec(bi.shape), const_spec(lam.shape),
                  const_spec(wpa.shape), const_spec(wpr.shape), const_spec(wout.shape),
                  const_spec(fg.shape)],
        out_specs=[col_spec(0), tail_spec, tail_spec],
        out_shape=[jax.ShapeDtypeStruct((batch * seq, D_MODEL), F32),
                   jax.ShapeDtypeStruct((batch, 8, D_RNN), F32),
                   jax.ShapeDtypeStruct((batch, 8, D_RNN), F32)],
        scratch_shapes=[pltpu.VMEM((8, D_RNN), F32), pltpu.VMEM((8, D_RNN), F32)],
        compiler_params=_params(2),
        name="post_prompt",
    )(x, o, p, p, p, p, p, cw, cb, wai, ba, bi, lam, wpa, wpr, wout, fg)


def _post_sample_kernel(x_ref, o_ref, za_ref, xr_ref, zr_ref, ga_ref, gr_ref, buf_ref, h0_ref,
                        cw_ref, cb_ref, wai_ref, ba_ref, bi_ref, lam_ref,
                        wpa_ref, wpr_ref, wout_ref, fg_ref,
                        y_ref, conv_ref, h_ref, *, final, n_seq, n_tok):
    xr = xr_ref[...].astype(F32)
    hist = [buf_ref[k] for k in range(CONV_W - 1)]
    hist += [xr[t * n_seq:(t + 1) * n_seq, :] for t in range(n_tok)]
    xc_parts = []
    for t in range(n_tok):
        acc = cb_ref[...]
        for j in range(CONV_W):
            acc = acc + cw_ref[j:j + 1, :] * hist[t + j]
        xc_parts.append(acc)
    xc = jnp.concatenate(xc_parts, axis=0)
    for k in range(CONV_W - 1):
        conv_ref[k] = hist[n_tok + k]

    r, i = _gates(xc, wai_ref, ba_ref, bi_ref)
    a, u = _lru_coeffs(xc, r, i, lam_ref)
    h = h0_ref[...]
    h_parts = []
    for t in range(n_tok):
        sl = slice(t * n_seq, (t + 1) * n_seq)
        h = a[sl, :] * h + u[sl, :]
        h_parts.append(h)
    h_ref[...] = h
    h_all = jnp.concatenate(h_parts, axis=0)

    y = _merge_out(x_ref[...], o_ref[...].astype(F32), za_ref[...].astype(F32), h_all,
                   zr_ref[...].astype(F32), ga_ref[...].astype(F32), gr_ref[...].astype(F32),
                   wpa_ref, wpr_ref, wout_ref)
    if final:
        y = _final_norm(y, fg_ref)
    y_ref[...] = y


def _post_sample(x, o, p, buf, h0, weights, final, n_seq, n_tok):
    n = n_seq * n_tok

    def col_spec(col):
        return pl.BlockSpec((n, D_MODEL), lambda i: (0, col))

    def const_spec(shape):
        return pl.BlockSpec(shape, lambda i: (0,) * len(shape))

    (cw, cb, wai, ba, bi, lam, wpa, wpr, wout, fg) = weights
    return pl.pallas_call(
        functools.partial(_post_sample_kernel, final=final, n_seq=n_seq, n_tok=n_tok),
        grid=(1,),
        in_specs=[col_spec(0), col_spec(0), col_spec(COL_ZA), col_spec(COL_XR), col_spec(COL_ZR),
                  col_spec(COL_GA), col_spec(COL_GR), const_spec(buf.shape), const_spec(h0.shape),
                  const_spec(cw.shape), const_spec(cb.shape), const_spec(wai.shape),
                  const_spec(ba.shape), const_spec(bi.shape), const_spec(lam.shape),
                  const_spec(wpa.shape), const_spec(wpr.shape), const_spec(wout.shape),
                  const_spec(fg.shape)],
        out_specs=[col_spec(0), const_spec(buf.shape), const_spec(h0.shape)],
        out_shape=[jax.ShapeDtypeStruct((n, D_MODEL), F32),
                   jax.ShapeDtypeStruct(buf.shape, F32),
                   jax.ShapeDtypeStruct(h0.shape, F32)],
        compiler_params=_params(1),
        name="post_sample",
    )(x, o, p, p, p, p, p, buf, h0, cw, cb, wai, ba, bi, lam, wpa, wpr, wout, fg)


def _t5_bucket(dist):
    n = jnp.maximum(dist, 0)
    max_exact = N_BUCKETS // 2
    nf = jnp.maximum(n, 1).astype(F32)
    large = max_exact + (jnp.log(nf / max_exact) / math.log(MAX_DIST / max_exact)
                         * (N_BUCKETS - max_exact)).astype(jnp.int32)
    large = jnp.minimum(large, N_BUCKETS - 1)
    return jnp.where(n < max_exact, n, large)


def _bias_lookup(rel_bias, dist):
    return rel_bias[_t5_bucket(dist)].astype(F32)


def _prompt_bias(rel_bias):
    TQ, TK = ATT_TQ, ATT_TK
    length = TQ + TK
    k = jnp.arange(length)
    rel = jnp.where(k < TQ, k, k - length)
    vecs = []
    for s in range(2):
        dist = TK * (1 - s) + rel
        vecs.append(jnp.where(dist[:, None] >= 0, _bias_lookup(rel_bias, dist), MASKED).T)
    far = rel_bias[N_BUCKETS - 1].astype(F32)
    return far, jnp.stack(vecs, axis=1)[:, :, None, :]


def _decode_bias(rel_bias, n_tok):
    def expand(b, valid):
        n_keys = b.shape[1]
        b = jnp.where(valid[:, :, None], b, MASKED)
        b = jnp.transpose(b, (0, 2, 1)).reshape(n_tok * N_HEADS, n_keys)
        b = jnp.repeat(b, N_HEADS, axis=1)
        hq = jnp.arange(n_tok * N_HEADS)[:, None] % N_HEADS
        hk = jnp.arange(n_keys * N_HEADS)[None, :] % N_HEADS
        b = jnp.where(hq == hk, b, MASKED)
        return jnp.concatenate([b, b], axis=0)

    i = jnp.arange(n_tok)[:, None]
    d_last = PAGE_SIZE + i - jnp.arange(PAGE_SIZE)[None, :]
    d_new = i - jnp.arange(n_tok)[None, :]
    far = expand(jnp.broadcast_to(rel_bias[N_BUCKETS - 1].astype(F32), (n_tok, PAGE_SIZE, N_HEADS)),
                 jnp.ones((n_tok, PAGE_SIZE), bool))
    last = expand(_bias_lookup(rel_bias, d_last), d_last >= 0)
    new = expand(_bias_lookup(rel_bias, d_new), d_new >= 0)
    return far, last, new


def kernel(x_prompt, x_sample, cache_k, cache_v, state_conv, state_h, page_table, rel_bias,
           norm_g, w_in, lam_vecs, subln_g, conv_w, conv_b, w_a, b_a, w_i, b_i, lru_lam,
           w_pa, w_pr, w_out, final_norm_g):
    batch, seq, _ = x_prompt.shape
    n_seq, n_tok, _ = x_sample.shape
    depth = w_in.shape[0]
    assert seq % ATT_TQ == 0 and ATT_TQ % ATT_TK == 0
    assert seq % POST_TILE == 0 and (batch * seq) % PROJ_TILE == 0
    assert page_table.shape[1] % PAGES_PER_STEP == 0 and cache_k.shape[2] == PAGE_SIZE
    assert ATT_TK >= MAX_DIST and PAGE_SIZE >= MAX_DIST

    cfar, bias_vecs = _prompt_bias(rel_bias)
    dec_far, dec_last, dec_new = _decode_bias(rel_bias, n_tok)

    yp = x_prompt.reshape(batch * seq, D_MODEL)
    ys = jnp.swapaxes(x_sample, 0, 1).reshape(n_tok * n_seq, D_MODEL)
    row = lambda v: v.reshape(1, -1)
    outs = [[] for _ in range(8)]

    for l in range(depth):
        lam_init = 0.8 - 0.6 * math.exp(-0.3 * l)
        final = l == depth - 1
        w_in_bf = w_in[l].astype(BF16)
        wai = jnp.concatenate([w_a[l], w_i[l]], axis=-1).astype(BF16)
        weights = (conv_w[l], row(conv_b[l]), wai, row(b_a[l]), row(b_i[l]), row(lru_lam[l]),
                   w_pa[l].astype(BF16), w_pr[l].astype(BF16), w_out[l].astype(BF16),
                   row(final_norm_g))
        g_in, sg = row(norm_g[l]), row(subln_g[l])

        p, kf, vf = _inproj(yp, g_in, w_in_bf, PROJ_TILE)
        o = _attn_prompt(p, cfar, bias_vecs, lam_vecs[l], sg, lam_init, batch, seq)
        yp, conv_t, h_t = _post_prompt(yp, o, p, weights, final, batch, seq)
        outs[0].append(kf.reshape(batch, seq, N_HEADS, V_DIM))
        outs[1].append(vf.reshape(batch, seq, N_HEADS, V_DIM))
        outs[2].append(conv_t[:, 8 - (CONV_W - 1):, :])
        outs[3].append(h_t[:, 7, :])

        ps, kfs, vfs = _inproj(ys, g_in, w_in_bf, n_tok * n_seq)
        seq_major = lambda v: jnp.swapaxes(v.reshape(n_tok, n_seq, N_HEADS, V_DIM), 0, 1)
        q_s = seq_major(ps[:, :D_MODEL]).reshape(n_seq, n_tok * N_HEADS, V_DIM)
        k_s, v_s = seq_major(kfs), seq_major(vfs)
        o_s = _attn_decode(l, page_table, q_s,
                           k_s.reshape(n_seq, n_tok * N_HEADS, V_DIM).astype(BF16),
                           v_s.reshape(n_seq, n_tok * N_HEADS, V_DIM).astype(BF16),
                           cache_k, cache_v, dec_far, dec_last, dec_new, lam_vecs[l], sg, lam_init)
        o_s = jnp.swapaxes(o_s.reshape(n_seq, n_tok, D_MODEL), 0, 1).reshape(n_tok * n_seq, D_MODEL)
        ys, conv_s, h_s = _post_sample(ys, o_s, ps, jnp.swapaxes(state_conv[l], 0, 1), state_h[l],
                                       weights, final, n_seq, n_tok)
        outs[4].append(k_s)
        outs[5].append(v_s)
        outs[6].append(jnp.swapaxes(conv_s, 0, 1))
        outs[7].append(h_s)

    y_prompt = yp.reshape(batch, seq, D_MODEL)
    y_sample = jnp.swapaxes(ys.reshape(n_tok, n_seq, D_MODEL), 0, 1)
    return (y_prompt, y_sample) + tuple(jnp.stack(o) for o in outs)
```

```python
import functools
import math

import jax
import jax.numpy as jnp
from jax import lax
from jax.experimental import pallas as pl
from jax.experimental.pallas import tpu as pltpu

F32 = jnp.float32
BF16 = jnp.bfloat16

D_MODEL = 1024
N_HEADS = 8
HEAD_DIM = 64
V_DIM = 2 * HEAD_DIM
D_RNN = 1024
N_BLOCKS = 8
BLOCK_W = D_RNN // N_BLOCKS
CONV_W = 4
RG_C = 8.0
N_BUCKETS = 32
MAX_DIST = 128
PAGE_SIZE = 128
EPS = 1e-6
ATT_SCALE = HEAD_DIM ** -0.5
N_PROJ = 8
COL_Q, COL_K, COL_V, COL_ZA, COL_XR, COL_ZR, COL_GA, COL_GR = range(N_PROJ)
MASKED = -1e30

ATT_TQ = 2048
ATT_TK = 512
ONES_ROWS = 16
SUBLANES = 8
POST_TILE = 256
PROJ_TILE = 1024
PAGES_PER_STEP = 16
DECODE_GROUPS = 1
VMEM_LIMIT = 56 * 1024 * 1024


def _params(n_axes):
    return pltpu.CompilerParams(dimension_semantics=("arbitrary",) * n_axes,
                                vmem_limit_bytes=VMEM_LIMIT)


def _sigmoid(x):
    return 1.0 / (1.0 + jnp.exp(-x))


def _silu(x):
    return x * _sigmoid(x)


def _softplus(x):
    return jnp.maximum(x, 0.0) + jnp.log1p(jnp.exp(-jnp.abs(x)))


def _diff_lambda(lv_ref, lam_init):
    a = jnp.sum(lv_ref[0:1, :] * lv_ref[1:2, :], axis=1, keepdims=True)
    b = jnp.sum(lv_ref[2:3, :] * lv_ref[3:4, :], axis=1, keepdims=True)
    return jnp.exp(a) - jnp.exp(b) + lam_init


def _split_maps(q):
    lane = lax.broadcasted_iota(jnp.int32, q.shape, 1)
    qs = q * jnp.asarray(ATT_SCALE, q.dtype)
    zero = jnp.zeros_like(qs)
    return jnp.concatenate([jnp.where(lane < HEAD_DIM, qs, zero),
                            jnp.where(lane >= HEAD_DIM, qs, zero)], axis=0)


def _subln(d, g, lam_init):
    ms = jnp.mean(d * d, axis=-1, keepdims=True)
    return d * lax.rsqrt(ms + EPS) * g * (1.0 - lam_init)


def _inproj_kernel(x_ref, g_ref, w_ref, p_ref, kf_ref, vf_ref, u_scr):
    j = pl.program_id(1)

    @pl.when(j == 0)
    def _():
        x = x_ref[...]
        ms = jnp.mean(x * x, axis=-1, keepdims=True)
        u_scr[...] = (x * lax.rsqrt(ms + EPS) * g_ref[...]).astype(BF16)

    acc = jnp.dot(u_scr[...], w_ref[...], preferred_element_type=F32)
    p_ref[...] = acc.astype(BF16)

    @pl.when(j == COL_K)
    def _():
        kf_ref[...] = acc

    @pl.when(j == COL_V)
    def _():
        vf_ref[...] = acc


def _inproj(x, g, w_bf, tile):
    n = x.shape[0]
    return pl.pallas_call(
        _inproj_kernel,
        grid=(n // tile, N_PROJ),
        in_specs=[
            pl.BlockSpec((tile, D_MODEL), lambda i, j: (i, 0)),
            pl.BlockSpec((1, D_MODEL), lambda i, j: (0, 0)),
            pl.BlockSpec((D_MODEL, D_MODEL), lambda i, j: (0, j)),
        ],
        out_specs=[
            pl.BlockSpec((tile, D_MODEL), lambda i, j: (i, j)),
            pl.BlockSpec((tile, D_MODEL), lambda i, j: (i, 0)),
            pl.BlockSpec((tile, D_MODEL), lambda i, j: (i, 0)),
        ],
        out_shape=[
            jax.ShapeDtypeStruct((n, N_PROJ * D_MODEL), BF16),
            jax.ShapeDtypeStruct((n, D_MODEL), F32),
            jax.ShapeDtypeStruct((n, D_MODEL), F32),
        ],
        scratch_shapes=[pltpu.VMEM((tile, D_MODEL), BF16)],
        compiler_params=_params(2),
        name="inproj",
    )(x, g, w_bf)


def _attn_prompt_kernel(cfar_ref, q_ref, k_ref, v_ref, bvec_ref, lv_ref, sg_ref,
                        o_ref, bias_ref, vte_scr, m_scr, acc_scr, *, lam_init, seq):
    TQ, TK = ATT_TQ, ATT_TK
    ratio = TQ // TK

    @pl.when(pl.program_id(1) == 0)
    def _():
        for s in range(bias_ref.shape[0]):
            vec = jnp.broadcast_to(bvec_ref[s], (TK, bvec_ref.shape[-1]))
            bias_ref[s] = pltpu.roll(vec, 0, 1, stride=1, stride_axis=0)[:, :TQ]

    for j in range(seq // TK):
        vt = v_ref[j * TK:(j + 1) * TK, :].astype(F32).T
        vte_scr[j, 0:V_DIM, :] = vt.astype(BF16)
        vte_scr[j, V_DIM:, :] = jnp.ones((ONES_ROWS, TK), BF16)

    lam = _diff_lambda(lv_ref, lam_init)
    c_far = cfar_ref[...]

    def q_tile(qi, carry):
        q_start = pl.multiple_of(qi * TQ, TQ)
        qq = _split_maps(q_ref[pl.ds(q_start, TQ), :])
        m_scr[...] = jnp.full(m_scr.shape, MASKED, F32)
        acc_scr[...] = jnp.zeros(acc_scr.shape, F32)

        def step(kj, bias_tile, lo):
            w = TQ - lo
            kt = k_ref[pl.ds(pl.multiple_of(kj * TK, TK), TK), :]
            qs = jnp.concatenate([qq[lo:TQ], qq[TQ + lo:]], axis=0)
            st = lax.dot_general(kt, qs, (((1,), (1,)), ((), ())), preferred_element_type=F32)
            m_old = jnp.concatenate([m_scr[0, :, lo:], m_scr[1, :, lo:]], axis=1)
            if bias_tile is None:
                m_new = jnp.maximum(m_old, jnp.max(st, axis=0, keepdims=True) + c_far)
                shift = m_new - c_far
            else:
                b = bias_tile[:, :w]
                st = st + jnp.concatenate([b, b], axis=1)
                m_new = jnp.maximum(m_old, jnp.max(st, axis=0, keepdims=True))
                shift = m_new
            alpha = jnp.exp(m_old - m_new)
            p = jnp.exp(st - shift).astype(BF16)
            pv = jnp.dot(vte_scr[kj], p, preferred_element_type=F32)
            for mp in range(2):
                cols = slice(mp * w, (mp + 1) * w)
                acc_scr[mp, :, lo:] = alpha[:, cols] * acc_scr[mp, :, lo:] + pv[:, cols]
                m_scr[mp, :, lo:] = m_new[:, cols]

        def far_body(kj, c):
            step(kj, None, 0)
            return c

        first_near = qi * ratio - 1
        lax.fori_loop(0, first_near, far_body, 0)

        @pl.when(qi >= 1)
        def _():
            step(first_near, bias_ref[0], 0)

        for s in range(ratio):
            step(qi * ratio + s, bias_ref[1], s * TK)

        o = [acc_scr[mp, 0:V_DIM, :] / acc_scr[mp, V_DIM:V_DIM + 1, :] for mp in range(2)]
        d = (o[0] - lam * o[1]).T
        o_ref[pl.ds(q_start, TQ), :] = _subln(d, sg_ref[...], lam_init).astype(o_ref.dtype)
        return carry

    lax.fori_loop(0, seq // TQ, q_tile, 0)


def _attn_prompt(p, cfar, bias_vecs, lam_vecs, subln_g, lam_init, batch, seq):
    TQ, TK = ATT_TQ, ATT_TK
    n_near, _, vec_len = bias_vecs.shape[1:]
    return pl.pallas_call(
        functools.partial(_attn_prompt_kernel, lam_init=lam_init, seq=seq),
        grid=(N_HEADS, batch),
        in_specs=[
            pl.BlockSpec((None, 1, 1), lambda h, b: (h, 0, 0)),
            pl.BlockSpec((seq, V_DIM), lambda h, b: (b, COL_Q * N_HEADS + h)),
            pl.BlockSpec((seq, V_DIM), lambda h, b: (b, COL_K * N_HEADS + h)),
            pl.BlockSpec((seq, V_DIM), lambda h, b: (b, COL_V * N_HEADS + h)),
            pl.BlockSpec((None, n_near, 1, vec_len), lambda h, b: (h, 0, 0, 0)),
            pl.BlockSpec((4, HEAD_DIM), lambda h, b: (0, 0)),
            pl.BlockSpec((1, V_DIM), lambda h, b: (0, 0)),
        ],
        out_specs=pl.BlockSpec((seq, V_DIM), lambda h, b: (b, h)),
        out_shape=jax.ShapeDtypeStruct((batch * seq, N_HEADS * V_DIM), BF16),
        scratch_shapes=[pltpu.VMEM((n_near, TK, TQ), F32),
                        pltpu.VMEM((seq // TK, V_DIM + ONES_ROWS, TK), BF16),
                        pltpu.VMEM((2, 1, TQ), F32),
                        pltpu.VMEM((2, V_DIM + ONES_ROWS, TQ), F32)],
        compiler_params=_params(2),
        name="attn_prompt",
    )(cfar.reshape(N_HEADS, 1, 1), p, p, p, bias_vecs, lam_vecs, subln_g)


def _attn_decode_kernel(pt_ref, q_ref, kn_ref, vn_ref, bfar_ref, blast_ref, bnew_ref, lv_ref, sg_ref,
                        *rest, lam_init):
    G = PAGES_PER_STEP
    k_refs = rest[:G]
    v_refs = rest[G:2 * G]
    o_ref = rest[2 * G]
    wq_scr, m_scr, l_scr, acc_scr = rest[2 * G + 1:]
    c = pl.program_id(1)
    n_c = pl.num_programs(1)
    rows = PAGE_SIZE * N_HEADS

    @pl.when(c == 0)
    def _():
        wq_scr[...] = _split_maps(q_ref[...])
        m_scr[...] = jnp.full(m_scr.shape, MASKED, F32)
        l_scr[...] = jnp.zeros(l_scr.shape, F32)
        acc_scr[...] = jnp.zeros(acc_scr.shape, F32)

    wq = wq_scr[...]

    def update(keys, values, biases):
        n_groups = min(DECODE_GROUPS, len(keys))
        size = len(keys) // n_groups
        parts = []
        for g in range(n_groups):
            blk = slice(g * size, (g + 1) * size)
            s = [lax.dot_general(wq, k, (((1,), (1,)), ((), ())), preferred_element_type=F32) + b
                 for k, b in zip(keys[blk], biases[blk])]
            m_grp = jnp.max(functools.reduce(jnp.maximum, s), axis=-1, keepdims=True)
            p = [jnp.exp(x - m_grp) for x in s]
            l_grp = jnp.sum(functools.reduce(jnp.add, p), axis=-1, keepdims=True)
            pv = [jnp.dot(x.astype(BF16), v, preferred_element_type=F32) for x, v in zip(p, values[blk])]
            parts.append((m_grp, l_grp, functools.reduce(jnp.add, pv)))
        m_old = m_scr[...]
        m_new = functools.reduce(jnp.maximum, [m_grp for m_grp, _, _ in parts], m_old)
        alpha = jnp.exp(m_old - m_new)
        l_sum = alpha * l_scr[...]
        acc = alpha * acc_scr[...]
        for m_grp, l_grp, a_grp in parts:
            w = jnp.exp(m_grp - m_new)
            l_sum = l_sum + w * l_grp
            acc = acc + w * a_grp
        l_scr[...] = l_sum
        acc_scr[...] = acc
        m_scr[...] = m_new

    bias_far = bfar_ref[...]
    update([k_refs[g][...].reshape(rows, V_DIM).astype(BF16) for g in range(G)],
           [v_refs[g][...].reshape(rows, V_DIM).astype(BF16) for g in range(G)],
           [bias_far] * (G - 1) + [jnp.where(c == n_c - 1, blast_ref[...], bias_far)])

    @pl.when(c == n_c - 1)
    def _():
        update([kn_ref[...]], [vn_ref[...]], [bnew_ref[...]])
        o = acc_scr[...] / l_scr[...]
        half = o.shape[0] // 2
        lam = _diff_lambda(lv_ref, lam_init)
        d = o[:half] - lam * o[half:]
        o_ref[...] = _subln(d, sg_ref[...], lam_init).astype(o_ref.dtype)


def _attn_decode(layer, page_table, q, k_new, v_new, cache_k, cache_v, bias_far, bias_last, bias_new,
                 lam_vecs, subln_g, lam_init):
    G = PAGES_PER_STEP
    n_seq, n_pages = page_table.shape
    tok_heads = q.shape[1]
    rows = PAGE_SIZE * N_HEADS

    def page_spec(g):
        return pl.BlockSpec((None, None, PAGE_SIZE, N_HEADS, V_DIM),
                            lambda s, c, pt: (layer, pt[s, c * G + g], 0, 0, 0))

    def seq_spec(r):
        return pl.BlockSpec((None, r, V_DIM), lambda s, c, pt: (s, 0, 0))

    def const_spec(shape):
        return pl.BlockSpec(shape, lambda s, c, pt: (0,) * len(shape))

    grid_spec = pltpu.PrefetchScalarGridSpec(
        num_scalar_prefetch=1,
        grid=(n_seq, n_pages // G),
        in_specs=[seq_spec(tok_heads), seq_spec(tok_heads), seq_spec(tok_heads),
                  const_spec((2 * tok_heads, rows)), const_spec((2 * tok_heads, rows)),
                  const_spec((2 * tok_heads, tok_heads)),
                  const_spec((4, HEAD_DIM)), const_spec((1, V_DIM))]
                 + [page_spec(g) for g in range(G)] + [page_spec(g) for g in range(G)],
        out_specs=seq_spec(tok_heads),
        scratch_shapes=[pltpu.VMEM((2 * tok_heads, V_DIM), BF16),
                        pltpu.VMEM((2 * tok_heads, 1), F32), pltpu.VMEM((2 * tok_heads, 1), F32),
                        pltpu.VMEM((2 * tok_heads, V_DIM), F32)],
    )
    return pl.pallas_call(
        functools.partial(_attn_decode_kernel, lam_init=lam_init),
        grid_spec=grid_spec,
        out_shape=jax.ShapeDtypeStruct((n_seq, tok_heads, V_DIM), BF16),
        compiler_params=_params(2),
        name="attn_decode",
    )(page_table, q, k_new, v_new, bias_far, bias_last, bias_new, lam_vecs, subln_g,
      *([cache_k] * G), *([cache_v] * G))


def _gates(xc, wai_ref, ba_ref, bi_ref):
    xb = xc.astype(BF16)
    r_parts, i_parts = [], []
    for n in range(N_BLOCKS):
        sl = slice(n * BLOCK_W, (n + 1) * BLOCK_W)
        g = jnp.dot(xb[:, sl], wai_ref[n], preferred_element_type=F32)
        r_parts.append(g[:, :BLOCK_W])
        i_parts.append(g[:, BLOCK_W:])
    r = _sigmoid(jnp.concatenate(r_parts, axis=1) + ba_ref[...])
    i = _sigmoid(jnp.concatenate(i_parts, axis=1) + bi_ref[...])
    return r, i


def _lru_coeffs(xc, r, i, lam_ref):
    log_a = -RG_C * r * _softplus(-lam_ref[...])
    a = jnp.exp(log_a)
    gap = 1.0 - a * a
    u = jnp.where(gap > 0.0, gap * lax.rsqrt(gap), 0.0) * (i * xc)
    return a, u


def _merge_out(x, o, za, h, zr, ga, gr, wpa_ref, wpr_ref, wout_ref):
    branch_a = jnp.dot((o * _silu(za)).astype(BF16), wpa_ref[...], preferred_element_type=F32)
    branch_r = jnp.dot((h * _silu(zr)).astype(BF16), wpr_ref[...], preferred_element_type=F32)
    merged = _sigmoid(ga) * branch_a + _sigmoid(gr) * branch_r
    return x + jnp.dot(merged.astype(BF16), wout_ref[...], preferred_element_type=F32)


def _final_norm(y, g_ref):
    ms = jnp.mean(y * y, axis=-1, keepdims=True)
    return y * lax.rsqrt(ms + EPS) * g_ref[...]


def _post_prompt_kernel(x_ref, o_ref, za_ref, xr_ref, zr_ref, ga_ref, gr_ref,
                        cw_ref, cb_ref, wai_ref, ba_ref, bi_ref, lam_ref,
                        wpa_ref, wpr_ref, wout_ref, fg_ref,
                        y_ref, conv_ref, h_ref, xprev_scr, hprev_scr, *, final):
    T = POST_TILE
    t = pl.program_id(1)

    @pl.when(t == 0)
    def _():
        xprev_scr[...] = jnp.zeros(xprev_scr.shape, F32)
        hprev_scr[...] = jnp.zeros(hprev_scr.shape, F32)

    xr = xr_ref[...].astype(F32)
    row8 = lax.broadcasted_iota(jnp.int32, (8, D_RNN), 0)
    xprev = xprev_scr[...]

    t_out = lax.broadcasted_iota(jnp.int32, (T, T), 0)
    t_in = lax.broadcasted_iota(jnp.int32, (T, T), 1)
    xb = xr_ref[...]

    def delayed(k):
        shift = (t_out - t_in == k).astype(BF16)
        r = jnp.dot(shift, xb, preferred_element_type=F32)
        top = jnp.where(row8 < k, pltpu.roll(xprev, k, 0), r[0:8])
        return jnp.concatenate([top, r[8:]], axis=0)

    xc = cb_ref[...] + cw_ref[3:4, :] * xr
    for k in range(1, CONV_W):
        xc = xc + cw_ref[CONV_W - 1 - k:CONV_W - k, :] * delayed(k)

    r, i = _gates(xc, wai_ref, ba_ref, bi_ref)
    a, u = _lru_coeffs(xc, r, i, lam_ref)

    grouped = (T // SUBLANES, SUBLANES, D_RNN)
    a = a.reshape(grouped)
    u = u.reshape(grouped)
    sub = lax.broadcasted_iota(jnp.int32, grouped, 1)
    d = 1
    while d < SUBLANES:
        keep = sub >= d
        u = jnp.where(keep, a * pltpu.roll(u, d, 1) + u, u)
        a = jnp.where(keep, a * pltpu.roll(a, d, 1), a)
        d *= 2
    a = a.reshape(T, D_RNN)
    u = u.reshape(T, D_RNN)
    carry = hprev_scr[SUBLANES - 1:SUBLANES, :]
    groups = []
    for g in range(T // SUBLANES):
        rows = slice(g * SUBLANES, (g + 1) * SUBLANES)
        hg = u[rows] + a[rows] * carry
        groups.append(hg)
        carry = hg[SUBLANES - 1:SUBLANES, :]
    h = jnp.concatenate(groups, axis=0)

    xprev_scr[...] = xr[T - 8:, :]
    hprev_scr[...] = h[T - 8:, :]
    conv_ref[...] = xr[T - 8:, :]
    h_ref[...] = h[T - 8:, :]

    y = _merge_out(x_ref[...], o_ref[...].astype(F32), za_ref[...].astype(F32), h,
                   zr_ref[...].astype(F32), ga_ref[...].astype(F32), gr_ref[...].astype(F32),
                   wpa_ref, wpr_ref, wout_ref)
    if final:
        y = _final_norm(y, fg_ref)
    y_ref[...] = y


def _post_prompt(x, o, p, weights, final, batch, seq):
    T = POST_TILE
    nt = seq // T

    def col_spec(col):
        return pl.BlockSpec((T, D_MODEL), lambda b, t: (b * nt + t, col))

    def const_spec(shape):
        return pl.BlockSpec(shape, lambda b, t: (0,) * len(shape))

    tail_spec = pl.BlockSpec((None, 8, D_RNN), lambda b, t: (b, 0, 0))
    (cw, cb, wai, ba, bi, lam, wpa, wpr, wout, fg) = weights
    return pl.pallas_call(
        functools.partial(_post_prompt_kernel, final=final),
        grid=(batch, nt),
        in_specs=[col_spec(0), col_spec(0), col_spec(COL_ZA), col_spec(COL_XR), col_spec(COL_ZR),
                  col_spec(COL_GA), col_spec(COL_GR),
                  const_spec(cw.shape), const_spec(cb.shape), const_spec(wai.shape),
                  const_spec(ba.shape), const_spec(bi.shape), const_spec(lam.shape),
                  const_spec(wpa.shape), const_spec(wpr.shape), const_spec(wout.shape),
                  const_spec(fg.shape)],
        out_specs=[col_spec(0), tail_spec, tail_spec],
        out_shape=[jax.ShapeDtypeStruct((batch * seq, D_MODEL), F32),
                   jax.ShapeDtypeStruct((batch, 8, D_RNN), F32),
                   jax.ShapeDtypeStruct((batch, 8, D_RNN), F32)],
        scratch_shapes=[pltpu.VMEM((8, D_RNN), F32), pltpu.VMEM((8, D_RNN), F32)],
        compiler_params=_params(2),
        name="post_prompt",
    )(x, o, p, p, p, p, p, cw, cb, wai, ba, bi, lam, wpa, wpr, wout, fg)


def _post_sample_kernel(x_ref, o_ref, za_ref, xr_ref, zr_ref, ga_ref, gr_ref, buf_ref, h0_ref,
                        cw_ref, cb_ref, wai_ref, ba_ref, bi_ref, lam_ref,
                        wpa_ref, wpr_ref, wout_ref, fg_ref,
                        y_ref, conv_ref, h_ref, *, final, n_seq, n_tok):
    xr = xr_ref[...].astype(F32)
    hist = [buf_ref[k] for k in range(CONV_W - 1)]
    hist += [xr[t * n_seq:(t + 1) * n_seq, :] for t in range(n_tok)]
    xc_parts = []
    for t in range(n_tok):
        acc = cb_ref[...]
        for j in range(CONV_W):
            acc = acc + cw_ref[j:j + 1, :] * hist[t + j]
        xc_parts.append(acc)
    xc = jnp.concatenate(xc_parts, axis=0)
    for k in range(CONV_W - 1):
        conv_ref[k] = hist[n_tok + k]

    r, i = _gates(xc, wai_ref, ba_ref, bi_ref)
    a, u = _lru_coeffs(xc, r, i, lam_ref)
    h = h0_ref[...]
    h_parts = []
    for t in range(n_tok):
        sl = slice(t * n_seq, (t + 1) * n_seq)
        h = a[sl, :] * h + u[sl, :]
        h_parts.append(h)
    h_ref[...] = h
    h_all = jnp.concatenate(h_parts, axis=0)

    y = _merge_out(x_ref[...], o_ref[...].astype(F32), za_ref[...].astype(F32), h_all,
                   zr_ref[...].astype(F32), ga_ref[...].astype(F32), gr_ref[...].astype(F32),
                   wpa_ref, wpr_ref, wout_ref)
    if final:
        y = _final_norm(y, fg_ref)
    y_ref[...] = y


def _post_sample(x, o, p, buf, h0, weights, final, n_seq, n_tok):
    n = n_seq * n_tok

    def col_spec(col):
        return pl.BlockSpec((n, D_MODEL), lambda i: (0, col))

    def const_spec(shape):
        return pl.BlockSpec(shape, lambda i: (0,) * len(shape))

    (cw, cb, wai, ba, bi, lam, wpa, wpr, wout, fg) = weights
    return pl.pallas_call(
        functools.partial(_post_sample_kernel, final=final, n_seq=n_seq, n_tok=n_tok),
        grid=(1,),
        in_specs=[col_spec(0), col_spec(0), col_spec(COL_ZA), col_spec(COL_XR), col_spec(COL_ZR),
                  col_spec(COL_GA), col_spec(COL_GR), const_spec(buf.shape), const_spec(h0.shape),
                  const_spec(cw.shape), const_spec(cb.shape), const_spec(wai.shape),
                  const_spec(ba.shape), const_spec(bi.shape), const_spec(lam.shape),
                  const_spec(wpa.shape), const_spec(wpr.shape), const_spec(wout.shape),
                  const_spec(fg.shape)],
        out_specs=[col_spec(0), const_spec(buf.shape), const_spec(h0.shape)],
        out_shape=[jax.ShapeDtypeStruct((n, D_MODEL), F32),
                   jax.ShapeDtypeStruct(buf.shape, F32),
                   jax.ShapeDtypeStruct(h0.shape, F32)],
        compiler_params=_params(1),
        name="post_sample",
    )(x, o, p, p, p, p, p, buf, h0, cw, cb, wai, ba, bi, lam, wpa, wpr, wout, fg)


def _t5_bucket(dist):
    n = jnp.maximum(dist, 0)
    max_exact = N_BUCKETS // 2
    nf = jnp.maximum(n, 1).astype(F32)
    large = max_exact + (jnp.log(nf / max_exact) / math.log(MAX_DIST / max_exact)
                         * (N_BUCKETS - max_exact)).astype(jnp.int32)
    large = jnp.minimum(large, N_BUCKETS - 1)
    return jnp.where(n < max_exact, n, large)


def _bias_lookup(rel_bias, dist):
    return rel_bias[_t5_bucket(dist)].astype(F32)


def _prompt_bias(rel_bias):
    TQ, TK = ATT_TQ, ATT_TK
    length = TQ + TK
    k = jnp.arange(length)
    rel = jnp.where(k < TQ, k, k - length)
    vecs = []
    for s in range(2):
        dist = TK * (1 - s) + rel
        vecs.append(jnp.where(dist[:, None] >= 0, _bias_lookup(rel_bias, dist), MASKED).T)
    far = rel_bias[N_BUCKETS - 1].astype(F32)
    return far, jnp.stack(vecs, axis=1)[:, :, None, :]


def _decode_bias(rel_bias, n_tok):
    def expand(b, valid):
        n_keys = b.shape[1]
        b = jnp.where(valid[:, :, None], b, MASKED)
        b = jnp.transpose(b, (0, 2, 1)).reshape(n_tok * N_HEADS, n_keys)
        b = jnp.repeat(b, N_HEADS, axis=1)
        hq = jnp.arange(n_tok * N_HEADS)[:, None] % N_HEADS
        hk = jnp.arange(n_keys * N_HEADS)[None, :] % N_HEADS
        b = jnp.where(hq == hk, b, MASKED)
        return jnp.concatenate([b, b], axis=0)

    i = jnp.arange(n_tok)[:, None]
    d_last = PAGE_SIZE + i - jnp.arange(PAGE_SIZE)[None, :]
    d_new = i - jnp.arange(n_tok)[None, :]
    far = expand(jnp.broadcast_to(rel_bias[N_BUCKETS - 1].astype(F32), (n_tok, PAGE_SIZE, N_HEADS)),
                 jnp.ones((n_tok, PAGE_SIZE), bool))
    last = expand(_bias_lookup(rel_bias, d_last), d_last >= 0)
    new = expand(_bias_lookup(rel_bias, d_new), d_new >= 0)
    return far, last, new


def kernel(x_prompt, x_sample, cache_k, cache_v, state_conv, state_h, page_table, rel_bias,
           norm_g, w_in, lam_vecs, subln_g, conv_w, conv_b, w_a, b_a, w_i, b_i, lru_lam,
           w_pa, w_pr, w_out, final_norm_g):
    batch, seq, _ = x_prompt.shape
    n_seq, n_tok, _ = x_sample.shape
    depth = w_in.shape[0]
    assert seq % ATT_TQ == 0 and ATT_TQ % ATT_TK == 0
    assert seq % POST_TILE == 0 and (batch * seq) % PROJ_TILE == 0
    assert page_table.shape[1] % PAGES_PER_STEP == 0 and cache_k.shape[2] == PAGE_SIZE
    assert ATT_TK >= MAX_DIST and PAGE_SIZE >= MAX_DIST

    cfar, bias_vecs = _prompt_bias(rel_bias)
    dec_far, dec_last, dec_new = _decode_bias(rel_bias, n_tok)

    yp = x_prompt.reshape(batch * seq, D_MODEL)
    ys = jnp.swapaxes(x_sample, 0, 1).reshape(n_tok * n_seq, D_MODEL)
    row = lambda v: v.reshape(1, -1)
    outs = [[] for _ in range(8)]

    for l in range(depth):
        lam_init = 0.8 - 0.6 * math.exp(-0.3 * l)
        final = l == depth - 1
        w_in_bf = w_in[l].astype(BF16)
        wai = jnp.concatenate([w_a[l], w_i[l]], axis=-1).astype(BF16)
        weights = (conv_w[l], row(conv_b[l]), wai, row(b_a[l]), row(b_i[l]), row(lru_lam[l]),
                   w_pa[l].astype(BF16), w_pr[l].astype(BF16), w_out[l].astype(BF16),
                   row(final_norm_g))
        g_in, sg = row(norm_g[l]), row(subln_g[l])

        p, kf, vf = _inproj(yp, g_in, w_in_bf, PROJ_TILE)
        o = _attn_prompt(p, cfar, bias_vecs, lam_vecs[l], sg, lam_init, batch, seq)
        yp, conv_t, h_t = _post_prompt(yp, o, p, weights, final, batch, seq)
        outs[0].append(kf.reshape(batch, seq, N_HEADS, V_DIM))
        outs[1].append(vf.reshape(batch, seq, N_HEADS, V_DIM))
        outs[2].append(conv_t[:, 8 - (CONV_W - 1):, :])
        outs[3].append(h_t[:, 7, :])

        ps, kfs, vfs = _inproj(ys, g_in, w_in_bf, n_tok * n_seq)
        seq_major = lambda v: jnp.swapaxes(v.reshape(n_tok, n_seq, N_HEADS, V_DIM), 0, 1)
        q_s = seq_major(ps[:, :D_MODEL]).reshape(n_seq, n_tok * N_HEADS, V_DIM)
        k_s, v_s = seq_major(kfs), seq_major(vfs)
        o_s = _attn_decode(l, page_table, q_s,
                           k_s.reshape(n_seq, n_tok * N_HEADS, V_DIM).astype(BF16),
                           v_s.reshape(n_seq, n_tok * N_HEADS, V_DIM).astype(BF16),
                           cache_k, cache_v, dec_far, dec_last, dec_new, lam_vecs[l], sg, lam_init)
        o_s = jnp.swapaxes(o_s.reshape(n_seq, n_tok, D_MODEL), 0, 1).reshape(n_tok * n_seq, D_MODEL)
        ys, conv_s, h_s = _post_sample(ys, o_s, ps, jnp.swapaxes(state_conv[l], 0, 1), state_h[l],
                                       weights, final, n_seq, n_tok)
        outs[4].append(k_s)
        outs[5].append(v_s)
        outs[6].append(jnp.swapaxes(conv_s, 0, 1))
        outs[7].append(h_s)

    y_prompt = yp.reshape(batch, seq, D_MODEL)
    y_sample = jnp.swapaxes(ys.reshape(n_tok, n_seq, D_MODEL), 0, 1)
    return (y_prompt, y_sample) + tuple(jnp.stack(o) for o in outs)
```
